```python
import math
import jax, jax.numpy as jnp
from jax import lax
import numpy as np

D_MODEL = 1024
BATCH = 4
SEQ = 8192
DEPTH = 1

SSM_WIDTH = D_MODEL // 2
SSM_GROUP = 16
SSM_GROUPS = SSM_WIDTH // SSM_GROUP
SSM_STATE = 64
DT_MIN = 1e-3
DT_MAX = 1e-1
ATTN_HEADS = 8
HEAD_DIM = 64
ATTN_WIDTH = ATTN_HEADS * HEAD_DIM
MOBA_BLOCK = 256
MOBA_TOPK = 3
Q_CHUNK = 32
N_BRANCH = 2
IN_WIDTH = SSM_WIDTH + 3 * ATTN_WIDTH + N_BRANCH * D_MODEL
D_FF = -(-8 * D_MODEL // (3 * 256)) * 256
N_MOD = 6
NORM_EPS = 1e-6

kernel_name = "hybrid_s5_moba_gated_sandwich_adaln"


def rms_norm(x, g):
    xf = x.astype(jnp.float32)
    y = xf * lax.rsqrt(jnp.mean(xf * xf, axis=-1, keepdims=True) + NORM_EPS)
    return (y * g.astype(jnp.float32)).astype(x.dtype)


def alibi_slopes(n_heads):
    return jnp.asarray(2.0 ** (-8.0 * np.arange(1, n_heads + 1) / n_heads), jnp.float32)


def s5_ssm(u, a_re, a_im, log_dt, b_re, b_im, c_re, c_im, d_skip):
    bsz, s, _ = u.shape
    f32 = jnp.float32
    uf = u.astype(f32).reshape(bsz, s, SSM_GROUPS, SSM_GROUP)
    a_re = a_re.astype(f32); a_im = a_im.astype(f32)
    dt = jnp.exp(log_dt.astype(f32))[:, None]
    mag = jnp.exp(dt * a_re)
    ab_re = mag * jnp.cos(dt * a_im)
    ab_im = mag * jnp.sin(dt * a_im)
    den = a_re * a_re + a_im * a_im
    nr = ab_re - 1.0
    co_re = (nr * a_re + ab_im * a_im) / den
    co_im = (ab_im * a_re - nr * a_im) / den
    b_re = b_re.astype(f32); b_im = b_im.astype(f32)
    bb_re = co_re[..., None] * b_re - co_im[..., None] * b_im
    bb_im = co_re[..., None] * b_im + co_im[..., None] * b_re
    bu_re = jnp.einsum('bsgc,gpc->sbgp', uf, bb_re)
    bu_im = jnp.einsum('bsgc,gpc->sbgp', uf, bb_im)
    a_seq_re = jnp.broadcast_to(ab_re[None, None], (s, 1) + ab_re.shape)
    a_seq_im = jnp.broadcast_to(ab_im[None, None], (s, 1) + ab_im.shape)

    def combine(left, right):
        ar_l, ai_l, br_l, bi_l = left
        ar_r, ai_r, br_r, bi_r = right
        ar = ar_r * ar_l - ai_r * ai_l
        ai = ar_r * ai_l + ai_r * ar_l
        br = ar_r * br_l - ai_r * bi_l + br_r
        bi = ar_r * bi_l + ai_r * br_l + bi_r
        return (ar, ai, br, bi)

    _, _, x_re, x_im = lax.associative_scan(combine, (a_seq_re, a_seq_im, bu_re, bu_im), axis=0)
    y = (jnp.einsum('sbgp,gcp->bsgc', x_re, c_re.astype(f32))
         - jnp.einsum('sbgp,gcp->bsgc', x_im, c_im.astype(f32))
         + d_skip.astype(f32) * uf)
    return y.reshape(bsz, s, SSM_WIDTH).astype(u.dtype)


def moba_attention(q, k, v):
    out_dtype = q.dtype
    f32 = jnp.float32
    bsz, h, s, dh = q.shape
    n_blk = -(-s // MOBA_BLOCK)
    s_pad = n_blk * MOBA_BLOCK
    pad = ((0, 0), (0, 0), (0, s_pad - s), (0, 0))
    q = jnp.pad(q.astype(f32), pad)
    k = jnp.pad(k.astype(f32), pad)
    v = jnp.pad(v.astype(f32), pad)
    kb = k.reshape(bsz, h, n_blk, MOBA_BLOCK, dh)
    vb = v.reshape(bsz, h, n_blk, MOBA_BLOCK, dh)
    k_mean = jnp.mean(kb, axis=3)
    q_blk = jnp.arange(s_pad) // MOBA_BLOCK
    gate = jnp.einsum('bhtd,bhnd->bhtn', q, k_mean)
    fully_past = jnp.arange(n_blk)[None, :] < q_blk[:, None]
    gate = jnp.where(fully_past, gate, -jnp.inf)
    top_k = min(MOBA_TOPK, n_blk)
    _, sel = lax.top_k(gate, top_k)

    n_chunk = s_pad // Q_CHUNK

    def to_chunks(a):
        a = a.reshape((bsz, h, n_chunk, Q_CHUNK) + a.shape[3:])
        return jnp.moveaxis(a, 2, 0)

    slopes = alibi_slopes(h)[None, :, None, None]
    b_ix = jnp.arange(bsz)[:, None, None, None]
    h_ix = jnp.arange(h)[None, :, None, None]
    offs = jnp.arange(MOBA_BLOCK)
    scale = HEAD_DIM ** -0.5
    n_sel = top_k * MOBA_BLOCK

    def chunk_attn(args):
        qc, selc, ci = args
        t = ci * Q_CHUNK + jnp.arange(Q_CHUNK)
        own = t[0] // MOBA_BLOCK
        kg = kb[b_ix, h_ix, selc].reshape(bsz, h, Q_CHUNK, n_sel, dh)
        vg = vb[b_ix, h_ix, selc].reshape(bsz, h, Q_CHUNK, n_sel, dh)
        s_sel = (selc[..., None] * MOBA_BLOCK + offs).reshape(bsz, h, Q_CHUNK, n_sel)
        ok_sel = jnp.broadcast_to((selc < own)[..., None], selc.shape + (MOBA_BLOCK,)).reshape(bsz, h, Q_CHUNK, n_sel)
        l_sel = jnp.einsum('bhcd,bhckd->bhck', qc, kg) * scale - slopes * (t[:, None] - s_sel)
        l_sel = jnp.where(ok_sel, l_sel, -jnp.inf)
        ko = lax.dynamic_slice_in_dim(k, own * MOBA_BLOCK, MOBA_BLOCK, axis=2)
        vo = lax.dynamic_slice_in_dim(v, own * MOBA_BLOCK, MOBA_BLOCK, axis=2)
        s_own = own * MOBA_BLOCK + offs
        dist_own = t[:, None] - s_own[None, :]
        l_own = jnp.einsum('bhcd,bhkd->bhck', qc, ko) * scale - slopes * dist_own
        l_own = jnp.where(dist_own >= 0, l_own, -jnp.inf)
        p = jax.nn.softmax(jnp.concatenate([l_sel, l_own], axis=-1), axis=-1)
        o = (jnp.einsum('bhck,bhckd->bhcd', p[..., :n_sel], vg)
             + jnp.einsum('bhck,bhkd->bhcd', p[..., n_sel:], vo))
        return o

    out = lax.map(chunk_attn, (to_chunks(q), to_chunks(sel), jnp.arange(n_chunk)))
    out = jnp.moveaxis(out, 0, 2).reshape(bsz, h, s_pad, dh)[:, :, :s]
    return out.astype(out_dtype)


def parallel_mixer(hn, w_in, a_re, a_im, log_dt, b_re, b_im, c_re, c_im, d_skip,
                   w_glu_a, w_glu_b, w_attn_out, w_out):
    bsz, s, _ = hn.shape
    proj = hn @ w_in
    cuts = [SSM_WIDTH, SSM_WIDTH + ATTN_WIDTH, SSM_WIDTH + 2 * ATTN_WIDTH, SSM_WIDTH + 3 * ATTN_WIDTH]
    u, q, k, v, g = jnp.split(proj, cuts, axis=-1)
    z = jax.nn.gelu(s5_ssm(u, a_re, a_im, log_dt, b_re, b_im, c_re, c_im, d_skip))
    y_a = (z @ w_glu_a) * jax.nn.sigmoid(z @ w_glu_b)
    to_heads = lambda t: t.reshape(bsz, s, ATTN_HEADS, HEAD_DIM).transpose(0, 2, 1, 3)
    o = moba_attention(to_heads(q), to_heads(k), to_heads(v))
    y_b = o.transpose(0, 2, 1, 3).reshape(bsz, s, ATTN_WIDTH) @ w_attn_out
    g_a, g_b = jnp.split(jax.nn.sigmoid(g), N_BRANCH, axis=-1)
    return (g_a * y_a + g_b * y_b) @ w_out


def swiglu_ffn(hn, w_gate, w_up, w_down):
    return (jax.nn.silu(hn @ w_gate) * (hn @ w_up)) @ w_down


def setup_inputs(seed: int = 0) -> dict:
    key = jax.random.key(seed)
    ks = jax.random.split(key, 24)
    f32 = jnp.float32
    nrm = lambda k, shape, s: jax.random.normal(k, shape, f32) * s
    L, G, P, C = DEPTH, SSM_GROUPS, SSM_STATE, SSM_GROUP
    n_idx = jnp.arange(P, dtype=f32)
    return {
        "x": nrm(ks[0], (BATCH, SEQ, D_MODEL), 1.0),
        "c": nrm(ks[1], (BATCH, D_MODEL), 1.0),
        "w_ada": nrm(ks[2], (L, D_MODEL, N_MOD * D_MODEL), 0.2 * D_MODEL ** -0.5),
        "b_ada": nrm(ks[3], (L, N_MOD * D_MODEL), 0.01),
        "g_pre_mix": 1.0 + nrm(ks[4], (L, D_MODEL), 0.02),
        "g_post_mix": 1.0 + nrm(ks[5], (L, D_MODEL), 0.02),
        "w_in": nrm(ks[6], (L, D_MODEL, IN_WIDTH), D_MODEL ** -0.5),
        "ssm_a_re": -0.5 * jnp.exp(nrm(ks[7], (L, G, P), 0.02)),
        "ssm_a_im": math.pi * n_idx + nrm(ks[8], (L, G, P), 0.01),
        "ssm_log_dt": jax.random.uniform(ks[9], (L, G), f32, math.log(DT_MIN), math.log(DT_MAX)),
        "ssm_b_re": nrm(ks[10], (L, G, P, C), (2 * C) ** -0.5),
        "ssm_b_im": nrm(ks[11], (L, G, P, C), (2 * C) ** -0.5),
        "ssm_c_re": nrm(ks[12], (L, G, C, P), P ** -0.5),
        "ssm_c_im": nrm(ks[13], (L, G, C, P), P ** -0.5),
        "ssm_d": nrm(ks[14], (L, G, C), 1.0),
        "w_glu_a": nrm(ks[15], (L, SSM_WIDTH, D_MODEL), SSM_WIDTH ** -0.5),
        "w_glu_b": nrm(ks[16], (L, SSM_WIDTH, D_MODEL), SSM_WIDTH ** -0.5),
        "w_attn_out": nrm(ks[17], (L, ATTN_WIDTH, D_MODEL), ATTN_WIDTH ** -0.5),
        "w_out": nrm(ks[18], (L, D_MODEL, D_MODEL), D_MODEL ** -0.5),
        "g_pre_ffn": 1.0 + nrm(ks[19], (L, D_MODEL), 0.02),
        "g_post_ffn": 1.0 + nrm(ks[20], (L, D_MODEL), 0.02),
        "w_ff_gate": nrm(ks[21], (L, D_MODEL, D_FF), D_MODEL ** -0.5),
        "w_ff_up": nrm(ks[22], (L, D_MODEL, D_FF), D_MODEL ** -0.5),
        "w_ff_down": nrm(ks[23], (L, D_FF, D_MODEL), D_FF ** -0.5),
    }


def reference(x, c, w_ada, b_ada, g_pre_mix, g_post_mix, w_in, ssm_a_re, ssm_a_im, ssm_log_dt,
              ssm_b_re, ssm_b_im, ssm_c_re, ssm_c_im, ssm_d, w_glu_a, w_glu_b, w_attn_out, w_out,
              g_pre_ffn, g_post_ffn, w_ff_gate, w_ff_up, w_ff_down):
    c_act = jax.nn.silu(c)
    for l in range(DEPTH):
        mod = c_act @ w_ada[l] + b_ada[l]
        sh_m, sc_m, gt_m, sh_f, sc_f, gt_f = [m[:, None, :] for m in jnp.split(mod, N_MOD, axis=-1)]
        hn = rms_norm(x, g_pre_mix[l]) * (1.0 + sc_m) + sh_m
        y = parallel_mixer(hn, w_in[l], ssm_a_re[l], ssm_a_im[l], ssm_log_dt[l], ssm_b_re[l], ssm_b_im[l],
                           ssm_c_re[l], ssm_c_im[l], ssm_d[l], w_glu_a[l], w_glu_b[l], w_attn_out[l], w_out[l])
        x = x + gt_m * rms_norm(y, g_post_mix[l])
        hn = rms_norm(x, g_pre_ffn[l]) * (1.0 + sc_f) + sh_f
        y = swiglu_ffn(hn, w_ff_gate[l], w_ff_up[l], w_ff_down[l])
        x = x + gt_f * rms_norm(y, g_post_ffn[l])
    return x
```

```python
import functools
import math

import jax
import jax.numpy as jnp
from jax import lax
from jax.experimental import pallas as pl
from jax.experimental.pallas import tpu as pltpu

F32 = jnp.float32
BF16 = jnp.bfloat16

SSM_GROUP = 16
SSM_STATE = 64
HEAD_DIM = 64
MOBA_BLOCK = 256
MOBA_TOPK = 3
N_MOD = 6
NORM_EPS = 1e-6
LANES = 128
SUBLANES = 8
GROUPS_PER_TILE = LANES // SSM_GROUP
NEG_BIG = -1e30
VMEM_LIMIT = 56 * 1024 * 1024


def _cparams(sem):
    return pltpu.CompilerParams(dimension_semantics=sem, vmem_limit_bytes=VMEM_LIMIT)


def _dot(a, b):
    return jnp.dot(a, b, preferred_element_type=F32)


def _adaln_kernel(c_ref, w_ref, b_ref, o_ref):
    c = c_ref[...]
    act = c * jax.nn.sigmoid(c)
    o_ref[...] = jnp.dot(act, w_ref[...], preferred_element_type=F32,
                         precision=lax.Precision.HIGHEST) + b_ref[...]


def _adaln(c, w_ada, b_ada):
    bsz, d = c.shape
    n = w_ada.shape[1]
    tn = n // N_MOD
    return pl.pallas_call(
        _adaln_kernel,
        out_shape=jax.ShapeDtypeStruct((bsz, n), F32),
        grid=(n // tn,),
        in_specs=[pl.BlockSpec((bsz, d), lambda j: (0, 0)),
                  pl.BlockSpec((d, tn), lambda j: (0, j)),
                  pl.BlockSpec((1, tn), lambda j: (0, j))],
        out_specs=pl.BlockSpec((bsz, tn), lambda j: (0, j)),
        compiler_params=_cparams(("arbitrary",)),
        name="adaln",
    )(c, w_ada, b_ada.reshape(1, n))


def _ssm_prep_kernel(are_ref, aim_ref, ldt_ref, bre_ref, bim_ref,
                     abre_ref, abim_ref, bbre_ref, bbim_ref):
    a_re = are_ref[...]
    a_im = aim_ref[...]
    dt = jnp.exp(ldt_ref[...])
    mag = jnp.exp(dt * a_re)
    ab_re = mag * jnp.cos(dt * a_im)
    ab_im = mag * jnp.sin(dt * a_im)
    den = a_re * a_re + a_im * a_im
    nr = ab_re - 1.0
    co_re = (nr * a_re + ab_im * a_im) / den
    co_im = (ab_im * a_re - nr * a_im) / den
    abre_ref[...] = ab_re
    abim_ref[...] = ab_im
    b_re = bre_ref[...]
    b_im = bim_ref[...]
    bbre_ref[...] = co_re[:, None, :] * b_re - co_im[:, None, :] * b_im
    bbim_ref[...] = co_re[:, None, :] * b_im + co_im[:, None, :] * b_re


def _ssm_prep(a_re, a_im, log_dt, b_re, b_im):
    g, p = a_re.shape
    c = b_re.shape[-1]
    bt_re = jnp.transpose(b_re, (0, 2, 1))
    bt_im = jnp.transpose(b_im, (0, 2, 1))
    return pl.pallas_call(
        _ssm_prep_kernel,
        out_shape=(jax.ShapeDtypeStruct((g, p), F32), jax.ShapeDtypeStruct((g, p), F32),
                   jax.ShapeDtypeStruct((g, c, p), F32), jax.ShapeDtypeStruct((g, c, p), F32)),
        name="ssm_prep",
    )(a_re, a_im, log_dt.reshape(g, 1), bt_re, bt_im)


def _block_diag_tiles(w):
    g, r, cc = w.shape
    nt = g // GROUPS_PER_TILE
    w = w.reshape(nt, GROUPS_PER_TILE, r, cc)
    eye = jnp.eye(GROUPS_PER_TILE, dtype=w.dtype)
    return jnp.einsum('sgrc,gh->sgrhc', w, eye).reshape(nt, GROUPS_PER_TILE * r, GROUPS_PER_TILE * cc)


def _proj_kernel(x_ref, mod_ref, gpre_ref, w_ref, place_ref,
                 u_ref, qa_ref, ka_ref, va_ref, sg_ref, kmt_ref, *, n_heads, n_blk):
    i = pl.program_id(1)
    tm = x_ref.shape[0]
    width = n_heads * HEAD_DIM
    ssm_w = u_ref.shape[1]
    blk_per_tile = tm // MOBA_BLOCK

    @pl.when(i == 0)
    def _():
        kmt_ref[...] = jnp.zeros_like(kmt_ref)

    x = x_ref[...]
    ms = jnp.mean(x * x, axis=-1, keepdims=True)
    hn = x * lax.rsqrt(ms + NORM_EPS) * gpre_ref[...]
    hn = hn * (1.0 + mod_ref[1:2, :]) + mod_ref[0:1, :]
    hb = hn.astype(BF16)

    c0 = 0
    u_ref[...] = _dot(hb, w_ref[:, c0:c0 + ssm_w]).astype(BF16)
    c0 += ssm_w
    q = _dot(hb, w_ref[:, c0:c0 + width])
    c0 += width
    k = _dot(hb, w_ref[:, c0:c0 + width])
    c0 += width
    v = _dot(hb, w_ref[:, c0:c0 + width])
    c0 += width
    sg_ref[...] = jax.nn.sigmoid(_dot(hb, w_ref[:, c0:])).astype(BF16)

    head_of_lane = lax.broadcasted_iota(jnp.int32, (n_heads, width), 1) // HEAD_DIM
    head_mask = (head_of_lane == lax.broadcasted_iota(jnp.int32, (n_heads, width), 0)).astype(F32)
    for lb in range(blk_per_tile):
        km = jnp.mean(k[lb * MOBA_BLOCK:(lb + 1) * MOBA_BLOCK, :], axis=0, keepdims=True)
        row0 = pl.multiple_of((i * blk_per_tile + lb) * n_heads, n_heads)
        kmt_ref[pl.ds(row0, n_heads), :] = km * head_mask

    gate = lax.dot_general(kmt_ref[...], q, (((1,), (1,)), ((), ())),
                           preferred_element_type=F32, precision=lax.Precision.HIGHEST)
    gate = gate.reshape(n_blk, n_heads, tm)
    n_iota = lax.broadcasted_iota(jnp.int32, (n_blk, n_heads, tm), 0)
    own = i * blk_per_tile + lax.broadcasted_iota(jnp.int32, (n_blk, n_heads, tm), 2) // MOBA_BLOCK
    neg_inf = jnp.float32(-jnp.inf)
    cur = jnp.where(n_iota < own, gate, neg_inf)
    sel = jnp.zeros((n_blk, n_heads, tm), F32)
    for _ in range(MOBA_TOPK):
        best = jnp.max(cur, axis=0, keepdims=True)
        idx = jnp.min(jnp.where(cur == best, n_iota, n_blk), axis=0, keepdims=True)
        hit = (n_iota == idx) & (best > neg_inf)
        sel = jnp.where(hit, 1.0, sel)
        cur = jnp.where(hit, neg_inf, cur)
    unsel = 1.0 - sel - (n_iota == own).astype(F32)
    unsel_t = jnp.transpose(unsel.reshape(n_blk * n_heads, tm)).astype(BF16)
    mask_feat = _dot(unsel_t, place_ref[...]) * NEG_BIG

    lane = lax.broadcasted_iota(jnp.int32, (tm, LANES), 1)
    own_row = i * blk_per_tile + lax.broadcasted_iota(jnp.int32, (tm, LANES), 0) // MOBA_BLOCK
    scale = HEAD_DIM ** -0.5
    for h in range(n_heads):
        tile = slice((h // 2) * LANES, (h // 2 + 1) * LANES)
        if h % 2 == 0:
            data_lanes = lane < HEAD_DIM
            feat0 = HEAD_DIM
        else:
            data_lanes = lane >= HEAD_DIM
            feat0 = 0
        mf = mask_feat[:, h * LANES:(h + 1) * LANES]
        qa_ref[h] = jnp.where(data_lanes, q[:, tile] * scale, mf).astype(BF16)
        onehot = (lane == feat0 + own_row).astype(F32)
        ka_ref[h] = jnp.where(data_lanes, k[:, tile], onehot).astype(BF16)
        ones_col = (lane == feat0).astype(F32)
        va_ref[h] = jnp.where(data_lanes, v[:, tile], ones_col).astype(BF16)


def _mask_placement(n_blk, n_heads):
    r = jnp.arange(n_blk * n_heads)
    n = r // n_heads
    h = r % n_heads
    col = h * LANES + jnp.where(h % 2 == 0, HEAD_DIM, 0) + n
    return (col[:, None] == jnp.arange(n_heads * LANES)[None, :]).astype(BF16)


def _proj(x, mod, g_pre, w_in, n_heads, ssm_w, tm):
    bsz, s, d = x.shape
    n_t = s // tm
    half_t = n_t // 2
    n_blk = s // MOBA_BLOCK
    width = n_heads * HEAD_DIM
    gate_w = w_in.shape[1] - ssm_w - 3 * width
    place = _mask_placement(n_blk, n_heads)
    kern = functools.partial(_proj_kernel, n_heads=n_heads, n_blk=n_blk)
    aug = jax.ShapeDtypeStruct((bsz, n_heads, s, LANES), BF16)
    aug_spec = pl.BlockSpec((None, n_heads, tm, LANES), lambda b, i: (b, 0, i, 0))
    return pl.pallas_call(
        kern,
        out_shape=(jax.ShapeDtypeStruct((s // 2, 2 * bsz * ssm_w), BF16), aug, aug, aug,
                   jax.ShapeDtypeStruct((bsz, s, gate_w), BF16)),
        grid=(bsz, n_t),
        in_specs=[pl.BlockSpec((None, tm, d), lambda b, i: (b, i, 0)),
                  pl.BlockSpec((None, N_MOD, d), lambda b, i: (b, 0, 0)),
                  pl.BlockSpec((1, d), lambda b, i: (0, 0)),
                  pl.BlockSpec(w_in.shape, lambda b, i: (0, 0)),
                  pl.BlockSpec(place.shape, lambda b, i: (0, 0))],
        out_specs=(pl.BlockSpec((tm, ssm_w), lambda b, i: (i % half_t, (i // half_t) * bsz + b)),
                   aug_spec, aug_spec, aug_spec,
                   pl.BlockSpec((None, tm, gate_w), lambda b, i: (b, i, 0))),
        scratch_shapes=[pltpu.VMEM((n_blk * n_heads, width), F32)],
        compiler_params=_cparams(("arbitrary", "arbitrary")),
        name="proj",
    )(x, mod, g_pre, w_in, place)


def _gelu_tanh(x):
    c = math.sqrt(2.0 / math.pi)
    return 0.5 * x * (1.0 + jnp.tanh(c * (x + 0.044715 * (x * x * x))))


def _ssm_kernel(u_ref, wb_ref, a_ref, wc_ref, d_ref, init_ref, *rest, emit):
    if emit:
        z_ref, fin_ref, bu_ref, st_ref = rest
    else:
        fin_ref, bu_ref, st_ref = rest
    i = pl.program_id(0)
    rows = u_ref.shape[0]
    steps = rows // SUBLANES
    n_tiles = wb_ref.shape[0]
    st_w = wb_ref.shape[2] // 2

    @pl.when(i == 0)
    def _():
        st_ref[...] = init_ref[...]

    for sg in range(n_tiles):
        r = _dot(u_ref[:, sg * LANES:(sg + 1) * LANES], wb_ref[sg])
        bu_ref[0, :, sg * st_w:(sg + 1) * st_w] = r[:, :st_w]
        bu_ref[1, :, sg * st_w:(sg + 1) * st_w] = r[:, st_w:]

    a_re = a_ref[0:1, :]
    a_im = a_ref[1:2, :]

    def step(t, carry):
        x_re, x_im = carry
        r0 = pl.multiple_of(t * SUBLANES, SUBLANES)
        n_re = a_re * x_re - a_im * x_im + bu_ref[0, pl.ds(r0, SUBLANES), :]
        n_im = a_re * x_im + a_im * x_re + bu_ref[1, pl.ds(r0, SUBLANES), :]
        bu_ref[0, pl.ds(r0, SUBLANES), :] = n_re
        bu_ref[1, pl.ds(r0, SUBLANES), :] = n_im
        return n_re, n_im

    x_re, x_im = lax.fori_loop(0, steps, step, (st_ref[0], st_ref[1]))
    st_ref[0] = x_re
    st_ref[1] = x_im
    fin_ref[0] = x_re
    fin_ref[1] = x_im

    if emit:
        for sg in range(n_tiles):
            xr = bu_ref[0, :, sg * st_w:(sg + 1) * st_w].astype(BF16)
            xi = bu_ref[1, :, sg * st_w:(sg + 1) * st_w].astype(BF16)
            y = _dot(xr, wc_ref[sg, :st_w, :]) + _dot(xi, wc_ref[sg, st_w:, :])
            lanes = slice(sg * LANES, (sg + 1) * LANES)
            y = y + d_ref[:, lanes] * u_ref[:, lanes].astype(F32)
            z_ref[:, lanes] = _gelu_tanh(y).astype(BF16)


def _ssm_pass(u_rows, wb, a_cat, wc, d_row, init, steps_per_tile, emit):
    rows_total, ssm_w = u_rows.shape
    rows = steps_per_tile * SUBLANES
    n_t = rows_total // rows
    st_lanes = a_cat.shape[1]
    fin_shape = jax.ShapeDtypeStruct((2, SUBLANES, st_lanes), F32)
    fin_spec = pl.BlockSpec((2, SUBLANES, st_lanes), lambda i: (0, 0, 0))
    if emit:
        out_shape = (jax.ShapeDtypeStruct((rows_total, ssm_w), BF16), fin_shape)
        out_specs = (pl.BlockSpec((rows, ssm_w), lambda i: (i, 0)), fin_spec)
    else:
        out_shape = fin_shape
        out_specs = fin_spec
    return pl.pallas_call(
        functools.partial(_ssm_kernel, emit=emit),
        out_shape=out_shape,
        grid=(n_t,),
        in_specs=[pl.BlockSpec((rows, ssm_w), lambda i: (i, 0)),
                  pl.BlockSpec(wb.shape, lambda i: (0, 0, 0)),
                  pl.BlockSpec(a_cat.shape, lambda i: (0, 0)),
                  pl.BlockSpec(wc.shape, lambda i: (0, 0, 0)),
                  pl.BlockSpec(d_row.shape, lambda i: (0, 0)),
                  pl.BlockSpec(init.shape, lambda i: (0, 0, 0))],
        out_specs=out_specs,
        scratch_shapes=[pltpu.VMEM((2, rows, st_lanes), F32),
                        pltpu.VMEM((2, SUBLANES, st_lanes), F32)],
        compiler_params=_cparams(("arbitrary",)),
        name="ssm_scan" if emit else "ssm_state",
    )(u_rows, wb, a_cat, wc, d_row, init)


def _attn_kernel(slope_ref, qa_ref, ka_ref, va_ref, o_ref, m_ref, acc_ref):
    hp = pl.program_id(1)
    i = pl.program_id(2)
    tq = qa_ref.shape[1]
    col = lax.broadcasted_iota(jnp.int32, (1, MOBA_BLOCK), 1)
    row = lax.broadcasted_iota(jnp.int32, (tq, 1), 0)
    lane = lax.broadcasted_iota(jnp.int32, (tq, LANES), 1)
    outs = []
    for hh in range(2):
        slope = slope_ref[hp * 2 + hh]
        q = qa_ref[hh]
        key_bias = slope * col.astype(F32)
        m_ref[...] = jnp.full(m_ref.shape, 0.1 * NEG_BIG, F32)
        acc_ref[...] = jnp.zeros_like(acc_ref)

        def update(s, v_blk):
            m_old = m_ref[...]
            m_new = jnp.maximum(m_old, jnp.max(s, axis=-1, keepdims=True))
            p = jnp.exp(s - m_new)
            acc_ref[...] = acc_ref[...] * jnp.exp(m_old - m_new) + _dot(p.astype(BF16), v_blk)
            m_ref[...] = m_new

        def past(j, carry):
            k0 = pl.multiple_of(j * MOBA_BLOCK, MOBA_BLOCK)
            k_blk = ka_ref[hh, pl.ds(k0, MOBA_BLOCK), :]
            s = lax.dot_general(q, k_blk, (((1,), (1,)), ((), ())), preferred_element_type=F32)
            shift = slope * ((j - i) * MOBA_BLOCK).astype(F32)
            update(s + (key_bias + shift), va_ref[hh, pl.ds(k0, MOBA_BLOCK), :])
            return carry

        lax.fori_loop(0, i, past, 0)

        k0 = pl.multiple_of(i * MOBA_BLOCK, MOBA_BLOCK)
        k_blk = ka_ref[hh, pl.ds(k0, MOBA_BLOCK), :]
        s = lax.dot_general(q, k_blk, (((1,), (1,)), ((), ())), preferred_element_type=F32)
        s = jnp.where(col <= row, s + key_bias, NEG_BIG)
        update(s, va_ref[hh, pl.ds(k0, MOBA_BLOCK), :])

        acc = acc_ref[...]
        l_col = HEAD_DIM if hh == 0 else 0
        outs.append(acc / acc[:, l_col:l_col + 1])
    o_ref[...] = jnp.where(lane < HEAD_DIM, outs[0], outs[1]).astype(o_ref.dtype)


def _attention(qa, ka, va):
    bsz, n_heads, s, _ = qa.shape
    n_q = s // MOBA_BLOCK
    slopes = jnp.asarray([2.0 ** (-8.0 * (h + 1) / n_heads) for h in range(n_heads)], F32)
    return pl.pallas_call(
        _attn_kernel,
        out_shape=jax.ShapeDtypeStruct((bsz, s, n_heads * HEAD_DIM), BF16),
        grid=(bsz, n_heads // 2, n_q),
        in_specs=[pl.BlockSpec(memory_space=pltpu.SMEM),
                  pl.BlockSpec((None, 2, MOBA_BLOCK, LANES), lambda b, hp, i: (b, hp, i, 0)),
                  pl.BlockSpec((None, 2, s, LANES), lambda b, hp, i: (b, hp, 0, 0)),
                  pl.BlockSpec((None, 2, s, LANES), lambda b, hp, i: (b, hp, 0, 0))],
        out_specs=pl.BlockSpec((None, MOBA_BLOCK, LANES), lambda b, hp, i: (b, i, hp)),
        scratch_shapes=[pltpu.VMEM((MOBA_BLOCK, 1), F32), pltpu.VMEM((MOBA_BLOCK, LANES), F32)],
        compiler_params=_cparams(("arbitrary", "arbitrary", "arbitrary")),
        name="attn",
    )(slopes, qa, ka, va)


def _rms(y, g):
    return y * lax.rsqrt(jnp.mean(y * y, axis=-1, keepdims=True) + NORM_EPS) * g


def _merge_kernel(x_ref, z_ref, o_ref, sg_ref, mod_ref, wa_ref, wb_ref, wo_ref, wout_ref, gpost_ref, out_ref):
    d = x_ref.shape[1]
    z = z_ref[...]
    y_a = _dot(z, wa_ref[...]) * jax.nn.sigmoid(_dot(z, wb_ref[...]))
    y_b = _dot(o_ref[...], wo_ref[...])
    merged = sg_ref[:, :d].astype(F32) * y_a + sg_ref[:, d:].astype(F32) * y_b
    y = _dot(merged.astype(BF16), wout_ref[...])
    out_ref[...] = x_ref[...] + mod_ref[2:3, :] * _rms(y, gpost_ref[...])


def _merge(x, z_tm, o, sg, mod, wa, wb, wo, wout, g_post, tm):
    bsz, s, d = x.shape
    n_t = s // tm
    half_t = n_t // 2
    ssm_w = wa.shape[0]
    full = lambda a: pl.BlockSpec(a.shape, lambda b, i: (0,) * a.ndim)
    return pl.pallas_call(
        _merge_kernel,
        out_shape=jax.ShapeDtypeStruct((bsz, s, d), F32),
        grid=(bsz, n_t),
        in_specs=[pl.BlockSpec((None, tm, d), lambda b, i: (b, i, 0)),
                  pl.BlockSpec((tm, ssm_w), lambda b, i: (i % half_t, (i // half_t) * bsz + b)),
                  pl.BlockSpec((None, tm, o.shape[2]), lambda b, i: (b, i, 0)),
                  pl.BlockSpec((None, tm, sg.shape[2]), lambda b, i: (b, i, 0)),
                  pl.BlockSpec((None, N_MOD, d), lambda b, i: (b, 0, 0)),
                  full(wa), full(wb), full(wo), full(wout), full(g_post)],
        out_specs=pl.BlockSpec((None, tm, d), lambda b, i: (b, i, 0)),
        compiler_params=_cparams(("arbitrary", "arbitrary")),
        name="merge",
    )(x, z_tm, o, sg, mod, wa, wb, wo, wout, g_post)


def _ffn_kernel(x_ref, mod_ref, gpre_ref, wg_ref, wu_ref, wd_ref, gpost_ref, out_ref, *, n_chunks):
    x = x_ref[...]
    hn = _rms(x, gpre_ref[...]) * (1.0 + mod_ref[4:5, :]) + mod_ref[3:4, :]
    hb = hn.astype(BF16)
    ff = wg_ref.shape[1]
    cw = ff // n_chunks
    y = None
    for c in range(n_chunks):
        cols = slice(c * cw, (c + 1) * cw)
        gate = _dot(hb, wg_ref[:, cols])
        up = _dot(hb, wu_ref[:, cols])
        mid = (gate * jax.nn.sigmoid(gate) * up).astype(BF16)
        part = _dot(mid, wd_ref[cols, :])
        y = part if y is None else y + part
    out_ref[...] = x + mod_ref[5:6, :] * _rms(y, gpost_ref[...])


def _ffn(x, mod, g_pre, wg, wu, wd, g_post, tm):
    bsz, s, d = x.shape
    ff = wg.shape[1]
    n_chunks = 2 if (ff // 2) % LANES == 0 else 1
    full = lambda a: pl.BlockSpec(a.shape, lambda b, i: (0,) * a.ndim)
    return pl.pallas_call(
        functools.partial(_ffn_kernel, n_chunks=n_chunks),
        out_shape=jax.ShapeDtypeStruct((bsz, s, d), F32),
        grid=(bsz, s // tm),
        in_specs=[pl.BlockSpec((None, tm, d), lambda b, i: (b, i, 0)),
                  pl.BlockSpec((None, N_MOD, d), lambda b, i: (b, 0, 0)),
                  full(g_pre), full(wg), full(wu), full(wd), full(g_post)],
        out_specs=pl.BlockSpec((None, tm, d), lambda b, i: (b, i, 0)),
        compiler_params=_cparams(("arbitrary", "arbitrary")),
        name="ffn",
    )(x, mod, g_pre, wg, wu, wd, g_post)


def _layer(x, mod, g_pre_mix, g_post_mix, w_in, ssm_a_re, ssm_a_im, ssm_log_dt, ssm_b_re, ssm_b_im,
           ssm_c_re, ssm_c_im, ssm_d, w_glu_a, w_glu_b, w_attn_out, w_out,
           g_pre_ffn, g_post_ffn, w_ff_gate, w_ff_up, w_ff_down):
    bsz, s, d = x.shape
    n_groups, n_state = ssm_a_re.shape
    ssm_w = n_groups * SSM_GROUP
    width = w_attn_out.shape[0]
    n_heads = width // HEAD_DIM
    assert 2 * bsz == SUBLANES, "SSM scan packs 2 sequence halves x batch into the 8 sublanes"
    tm = min(512, s // 2)
    steps_per_tile = min(64, s // 2)
    assert s % (2 * tm) == 0 and tm % MOBA_BLOCK == 0 and (s // 2) % steps_per_tile == 0

    mod3 = mod.reshape(bsz, N_MOD, d)
    row = lambda g: g.reshape(1, -1)

    ab_re, ab_im, bbt_re, bbt_im = _ssm_prep(ssm_a_re, ssm_a_im, ssm_log_dt, ssm_b_re, ssm_b_im)
    wb = jnp.concatenate([_block_diag_tiles(bbt_re), _block_diag_tiles(bbt_im)], axis=-1).astype(BF16)
    wc = jnp.concatenate([_block_diag_tiles(jnp.transpose(ssm_c_re, (0, 2, 1))),
                          _block_diag_tiles(-jnp.transpose(ssm_c_im, (0, 2, 1)))], axis=1).astype(BF16)
    a_cat = jnp.stack([ab_re.reshape(-1), ab_im.reshape(-1)], axis=0)
    d_row = ssm_d.reshape(1, ssm_w)

    u_tm, qa, ka, va, sg = _proj(x, mod3, row(g_pre_mix), w_in.astype(BF16), n_heads, ssm_w, tm)

    u_rows = u_tm.reshape((s // 2) * SUBLANES, ssm_w)
    zero_state = jnp.zeros((2, SUBLANES, n_groups * n_state), F32)
    first = _ssm_pass(u_rows, wb, a_cat, wc, d_row, zero_state, steps_per_tile, emit=False)
    init = jnp.concatenate([jnp.zeros_like(first[:, :bsz]), first[:, :bsz]], axis=1)
    z_rows, _ = _ssm_pass(u_rows, wb, a_cat, wc, d_row, init, steps_per_tile, emit=True)
    z_tm = z_rows.reshape(s // 2, SUBLANES * ssm_w)

    o = _attention(qa, ka, va)

    x1 = _merge(x, z_tm, o, sg, mod3, w_glu_a.astype(BF16), w_glu_b.astype(BF16),
                w_attn_out.astype(BF16), w_out.astype(BF16), row(g_post_mix), tm)
    return _ffn(x1, mod3, row(g_pre_ffn), w_ff_gate.astype(BF16), w_ff_up.astype(BF16),
                w_ff_down.astype(BF16), row(g_post_ffn), tm)


def kernel(x, c, w_ada, b_ada, g_pre_mix, g_post_mix, w_in, ssm_a_re, ssm_a_im, ssm_log_dt, ssm_b_re, ssm_b_im, ssm_c_re, ssm_c_im, ssm_d, w_glu_a, w_glu_b, w_attn_out, w_out, g_pre_ffn, g_post_ffn, w_ff_gate, w_ff_up, w_ff_down):
    depth = w_ada.shape[0]
    for l in range(depth):
        mod = _adaln(c, w_ada[l], b_ada[l])
        x = _layer(x, mod, g_pre_mix[l], g_post_mix[l], w_in[l], ssm_a_re[l], ssm_a_im[l], ssm_log_dt[l],
                   ssm_b_re[l], ssm_b_im[l], ssm_c_re[l], ssm_c_im[l], ssm_d[l], w_glu_a[l], w_glu_b[l],
                   w_attn_out[l], w_out[l], g_pre_ffn[l], g_post_ffn[l], w_ff_gate[l], w_ff_up[l], w_ff_down[l])
    return x
```

```python
import functools
import math

import jax
import jax.numpy as jnp
from jax import lax
from jax.experimental import pallas as pl
from jax.experimental.pallas import tpu as pltpu

F32 = jnp.float32
BF16 = jnp.bfloat16

SSM_GROUP = 16
SSM_STATE = 64
HEAD_DIM = 64
MOBA_BLOCK = 256
MOBA_TOPK = 3
N_MOD = 6
NORM_EPS = 1e-6
LANES = 128
SUBLANES = 8
GROUPS_PER_TILE = LANES // SSM_GROUP
NEG_BIG = -1e30
MASK_FEATS = 32
V_ROWS = HEAD_DIM + 16
KV_TILE = 2 * MOBA_BLOCK
VMEM_LIMIT = 56 * 1024 * 1024


def _cparams(sem):
    return pltpu.CompilerParams(dimension_semantics=sem, vmem_limit_bytes=VMEM_LIMIT)


def _dot(a, b):
    return jnp.dot(a, b, preferred_element_type=F32)


def _dot_nt(a, b):
    return lax.dot_general(a, b, (((1,), (1,)), ((), ())), preferred_element_type=F32)


def _adaln_kernel(c_ref, w_ref, b_ref, o_ref):
    c = c_ref[...]
    act = c * jax.nn.sigmoid(c)
    o_ref[...] = jnp.dot(act, w_ref[...], preferred_element_type=F32,
                         precision=lax.Precision.HIGHEST) + b_ref[...]


def _adaln(c, w_ada, b_ada):
    bsz, d = c.shape
    n = w_ada.shape[1]
    tn = n // N_MOD
    return pl.pallas_call(
        _adaln_kernel,
        out_shape=jax.ShapeDtypeStruct((bsz, n), F32),
        grid=(n // tn,),
        in_specs=[pl.BlockSpec((bsz, d), lambda j: (0, 0)),
                  pl.BlockSpec((d, tn), lambda j: (0, j)),
                  pl.BlockSpec((1, tn), lambda j: (0, j))],
        out_specs=pl.BlockSpec((bsz, tn), lambda j: (0, j)),
        compiler_params=_cparams(("arbitrary",)),
        name="adaln",
    )(c, w_ada, b_ada.reshape(1, n))


def _ssm_prep_kernel(are_ref, aim_ref, ldt_ref, bre_ref, bim_ref,
                     abre_ref, abim_ref, bbre_ref, bbim_ref):
    a_re = are_ref[...]
    a_im = aim_ref[...]
    dt = jnp.exp(ldt_ref[...])
    mag = jnp.exp(dt * a_re)
    ab_re = mag * jnp.cos(dt * a_im)
    ab_im = mag * jnp.sin(dt * a_im)
    den = a_re * a_re + a_im * a_im
    nr = ab_re - 1.0
    co_re = (nr * a_re + ab_im * a_im) / den
    co_im = (ab_im * a_re - nr * a_im) / den
    abre_ref[...] = ab_re
    abim_ref[...] = ab_im
    b_re = bre_ref[...]
    b_im = bim_ref[...]
    bbre_ref[...] = co_re[:, None, :] * b_re - co_im[:, None, :] * b_im
    bbim_ref[...] = co_re[:, None, :] * b_im + co_im[:, None, :] * b_re


def _ssm_prep(a_re, a_im, log_dt, b_re, b_im):
    g, p = a_re.shape
    c = b_re.shape[-1]
    bt_re = jnp.transpose(b_re, (0, 2, 1))
    bt_im = jnp.transpose(b_im, (0, 2, 1))
    return pl.pallas_call(
        _ssm_prep_kernel,
        out_shape=(jax.ShapeDtypeStruct((g, p), F32), jax.ShapeDtypeStruct((g, p), F32),
                   jax.ShapeDtypeStruct((g, c, p), F32), jax.ShapeDtypeStruct((g, c, p), F32)),
        name="ssm_prep",
    )(a_re, a_im, log_dt.reshape(g, 1), bt_re, bt_im)


def _block_diag_tiles(w):
    g, r, cc = w.shape
    nt = g // GROUPS_PER_TILE
    w = w.reshape(nt, GROUPS_PER_TILE, r, cc)
    eye = jnp.eye(GROUPS_PER_TILE, dtype=w.dtype)
    return jnp.einsum('sgrc,gh->sgrhc', w, eye).reshape(nt, GROUPS_PER_TILE * r, GROUPS_PER_TILE * cc)


def _alibi_slopes(n_heads):
    return [2.0 ** (-8.0 * (h + 1) / n_heads) for h in range(n_heads)]


def _proj_kernel(x_ref, mod_ref, gpre_ref, wu_ref, wqt_ref, wk_ref, wvt_ref, wg_ref, perm_ref,
                 u_ref, qt_ref, ka_ref, vt_ref, sg_ref, kmt_ref, *, n_heads, n_blk):
    i = pl.program_id(1)
    tm = x_ref.shape[0]
    width = n_heads * HEAD_DIM
    blk_per_tile = tm // MOBA_BLOCK

    @pl.when(i == 0)
    def _():
        kmt_ref[...] = jnp.zeros_like(kmt_ref)

    x = x_ref[...]
    ms = jnp.mean(x * x, axis=-1, keepdims=True)
    hn = x * lax.rsqrt(ms + NORM_EPS) * gpre_ref[...]
    hn = hn * (1.0 + mod_ref[1:2, :]) + mod_ref[0:1, :]
    hb = hn.astype(BF16)

    u_ref[...] = _dot(hb, wu_ref[...]).astype(BF16)
    q_t = _dot_nt(wqt_ref[...], hb)
    k = _dot(hb, wk_ref[...])
    v_t = _dot_nt(wvt_ref[...], hb)
    sg_ref[...] = jax.nn.sigmoid(_dot(hb, wg_ref[...])).astype(BF16)

    head_of_lane = lax.broadcasted_iota(jnp.int32, (n_heads, width), 1) // HEAD_DIM
    head_mask = (head_of_lane == lax.broadcasted_iota(jnp.int32, (n_heads, width), 0)).astype(F32)
    for lb in range(blk_per_tile):
        km = jnp.mean(k[lb * MOBA_BLOCK:(lb + 1) * MOBA_BLOCK, :], axis=0, keepdims=True)
        row0 = pl.multiple_of((i * blk_per_tile + lb) * n_heads, n_heads)
        kmt_ref[pl.ds(row0, n_heads), :] = km * head_mask

    gate = jnp.dot(kmt_ref[...], q_t, preferred_element_type=F32, precision=lax.Precision.HIGHEST)
    gate = gate.reshape(n_blk, n_heads, tm)
    n_iota = lax.broadcasted_iota(jnp.int32, (n_blk, n_heads, tm), 0)
    own = i * blk_per_tile + lax.broadcasted_iota(jnp.int32, (n_blk, n_heads, tm), 2) // MOBA_BLOCK
    neg_inf = jnp.float32(-jnp.inf)
    cur = jnp.where(n_iota < own, gate, neg_inf)
    sel = jnp.zeros((n_blk, n_heads, tm), F32)
    for _ in range(MOBA_TOPK):
        best = jnp.max(cur, axis=0, keepdims=True)
        idx = jnp.min(jnp.where(cur == best, n_iota, n_blk), axis=0, keepdims=True)
        hit = (n_iota == idx) & (best > neg_inf)
        sel = jnp.where(hit, 1.0, sel)
        cur = jnp.where(hit, neg_inf, cur)
    unsel = 1.0 - sel - (n_iota == own).astype(F32)
    unsel = unsel.reshape(n_blk * n_heads, tm).astype(BF16)
    mask_rows = _dot(perm_ref[...], unsel) * NEG_BIG

    sub8 = lax.broadcasted_iota(jnp.int32, (SUBLANES, tm), 0)
    ones2 = (sub8 < 2).astype(F32)
    ones1 = (sub8 < 1).astype(F32)
    zeros = lambda r: jnp.zeros((r, tm), F32)
    lane = lax.broadcasted_iota(jnp.int32, (tm, LANES), 1)
    row = lax.broadcasted_iota(jnp.int32, (tm, LANES), 0)
    own_row = i * blk_per_tile + row // MOBA_BLOCK
    pos = (row % MOBA_BLOCK).astype(F32)
    scale = HEAD_DIM ** -0.5
    slopes = _alibi_slopes(n_heads)
    pad_feats = LANES - HEAD_DIM - MASK_FEATS - SUBLANES
    for h in range(n_heads):
        rows = slice(h * HEAD_DIM, (h + 1) * HEAD_DIM)
        tile = slice((h // 2) * LANES, (h // 2 + 1) * LANES)
        q_rows = q_t[rows] * scale
        m_rows = mask_rows[h * n_blk:(h + 1) * n_blk]
        if n_blk < MASK_FEATS:
            m_rows = jnp.concatenate([m_rows, zeros(MASK_FEATS - n_blk)], axis=0)
        bias = slopes[h] * pos
        bias_hi = bias.astype(BF16).astype(F32)
        bias_lo = bias - bias_hi
        if h % 2 == 0:
            qt = jnp.concatenate([q_rows, m_rows, ones2, zeros(pad_feats)], axis=0)
            data_lanes = lane < HEAD_DIM
            feat0 = HEAD_DIM
        else:
            qt = jnp.concatenate([m_rows, ones2, zeros(pad_feats), q_rows], axis=0)
            data_lanes = lane >= HEAD_DIM
            feat0 = 0
        qt_ref[h] = qt.astype(BF16)
        feats = jnp.where(lane == feat0 + MASK_FEATS, bias_hi,
                          jnp.where(lane == feat0 + MASK_FEATS + 1, bias_lo,
                                    (lane == feat0 + own_row).astype(F32)))
        ka = jnp.where(data_lanes, k[:, tile], feats).astype(BF16)
        vt = jnp.concatenate([v_t[rows], ones1, zeros(V_ROWS - HEAD_DIM - SUBLANES)], axis=0).astype(BF16)
        for kt in range(tm // KV_TILE):
            ka_ref[h, kt] = ka[kt * KV_TILE:(kt + 1) * KV_TILE]
            vt_ref[h, kt] = vt[:, kt * KV_TILE:(kt + 1) * KV_TILE]


def _row_permutation(n_blk, n_heads):
    r = jnp.arange(n_heads * n_blk)
    src = (r % n_blk) * n_heads + r // n_blk
    return (src[:, None] == jnp.arange(n_blk * n_heads)[None, :]).astype(BF16)


def _proj(x, mod, g_pre, w_in, n_heads, ssm_w, tm):
    bsz, s, d = x.shape
    n_t = s // tm
    half_t = n_t // 2
    n_blk = s // MOBA_BLOCK
    width = n_heads * HEAD_DIM
    assert n_blk <= MASK_FEATS and tm % KV_TILE == 0
    c = [0, ssm_w, ssm_w + width, ssm_w + 2 * width, ssm_w + 3 * width, w_in.shape[1]]
    gate_w = c[5] - c[4]
    wu = w_in[:, c[0]:c[1]].astype(BF16)
    wqt = w_in[:, c[1]:c[2]].T.astype(BF16)
    wk = w_in[:, c[2]:c[3]].astype(BF16)
    wvt = w_in[:, c[3]:c[4]].T.astype(BF16)
    wg = w_in[:, c[4]:c[5]].astype(BF16)
    perm = _row_permutation(n_blk, n_heads)
    full = lambda a: pl.BlockSpec(a.shape, lambda b, i: (0,) * a.ndim)
    kern = functools.partial(_proj_kernel, n_heads=n_heads, n_blk=n_blk)
    return pl.pallas_call(
        kern,
        out_shape=(jax.ShapeDtypeStruct((s // 2, 2 * bsz * ssm_w), BF16),
                   jax.ShapeDtypeStruct((bsz, n_heads, LANES, s), BF16),
                   jax.ShapeDtypeStruct((bsz, n_heads, s // KV_TILE, KV_TILE, LANES), BF16),
                   jax.ShapeDtypeStruct((bsz, n_heads, s // KV_TILE, V_ROWS, KV_TILE), BF16),
                   jax.ShapeDtypeStruct((bsz, s, gate_w), BF16)),
        grid=(bsz, n_t),
        in_specs=[pl.BlockSpec((None, tm, d), lambda b, i: (b, i, 0)),
                  pl.BlockSpec((None, N_MOD, d), lambda b, i: (b, 0, 0)),
                  full(g_pre), full(wu), full(wqt), full(wk), full(wvt), full(wg), full(perm)],
        out_specs=(pl.BlockSpec((tm, ssm_w), lambda b, i: (i % half_t, (i // half_t) * bsz + b)),
                   pl.BlockSpec((None, n_heads, LANES, tm), lambda b, i: (b, 0, 0, i)),
                   pl.BlockSpec((None, n_heads, tm // KV_TILE, KV_TILE, LANES), lambda b, i: (b, 0, i, 0, 0)),
                   pl.BlockSpec((None, n_heads, tm // KV_TILE, V_ROWS, KV_TILE), lambda b, i: (b, 0, i, 0, 0)),
                   pl.BlockSpec((None, tm, gate_w), lambda b, i: (b, i, 0))),
        scratch_shapes=[pltpu.VMEM((n_blk * n_heads, width), F32)],
        compiler_params=_cparams(("arbitrary", "arbitrary")),
        name="proj",
    )(x, mod, g_pre, wu, wqt, wk, wvt, wg, perm)


def _gelu_tanh(x):
    c = math.sqrt(2.0 / math.pi)
    return 0.5 * x * (1.0 + jnp.tanh(c * (x + 0.044715 * (x * x * x))))


def _ssm_kernel(u_ref, wb_ref, a_ref, wc_ref, d_ref, init_ref, *rest, emit):
    if emit:
        z_ref, fin_ref, bu_ref, st_ref = rest
    else:
        fin_ref, bu_ref, st_ref = rest
    i = pl.program_id(0)
    rows = u_ref.shape[0]
    steps = rows // SUBLANES
    n_tiles = wb_ref.shape[0]
    st_w = wb_ref.shape[2] // 2

    @pl.when(i == 0)
    def _():
        st_ref[...] = init_ref[...]

    for sg in range(n_tiles):
        r = _dot(u_ref[:, sg * LANES:(sg + 1) * LANES], wb_ref[sg])
        bu_ref[0, :, sg * st_w:(sg + 1) * st_w] = r[:, :st_w]
        bu_ref[1, :, sg * st_w:(sg + 1) * st_w] = r[:, st_w:]

    a_re = a_ref[0:1, :]
    a_im = a_ref[1:2, :]

    def step(t, carry):
        x_re, x_im = carry
        r0 = pl.multiple_of(t * SUBLANES, SUBLANES)
        n_re = a_re * x_re - a_im * x_im + bu_ref[0, pl.ds(r0, SUBLANES), :]
        n_im = a_re * x_im + a_im * x_re + bu_ref[1, pl.ds(r0, SUBLANES), :]
        bu_ref[0, pl.ds(r0, SUBLANES), :] = n_re
        bu_ref[1, pl.ds(r0, SUBLANES), :] = n_im
        return n_re, n_im

    x_re, x_im = lax.fori_loop(0, steps, step, (st_ref[0], st_ref[1]))
    st_ref[0] = x_re
    st_ref[1] = x_im
    fin_ref[0] = x_re
    fin_ref[1] = x_im

    if emit:
        for sg in range(n_tiles):
            xr = bu_ref[0, :, sg * st_w:(sg + 1) * st_w].astype(BF16)
            xi = bu_ref[1, :, sg * st_w:(sg + 1) * st_w].astype(BF16)
            y = _dot(xr, wc_ref[sg, :st_w, :]) + _dot(xi, wc_ref[sg, st_w:, :])
            lanes = slice(sg * LANES, (sg + 1) * LANES)
            y = y + d_ref[:, lanes] * u_ref[:, lanes].astype(F32)
            z_ref[:, lanes] = _gelu_tanh(y).astype(BF16)


def _ssm_pass(u_rows, wb, a_cat, wc, d_row, init, steps_per_tile, emit):
    rows_total, ssm_w = u_rows.shape
    rows = steps_per_tile * SUBLANES
    n_t = rows_total // rows
    st_lanes = a_cat.shape[1]
    fin_shape = jax.ShapeDtypeStruct((2, SUBLANES, st_lanes), F32)
    fin_spec = pl.BlockSpec((2, SUBLANES, st_lanes), lambda i: (0, 0, 0))
    if emit:
        out_shape = (jax.ShapeDtypeStruct((rows_total, ssm_w), BF16), fin_shape)
        out_specs = (pl.BlockSpec((rows, ssm_w), lambda i: (i, 0)), fin_spec)
    else:
        out_shape = fin_shape
        out_specs = fin_spec
    return pl.pallas_call(
        functools.partial(_ssm_kernel, emit=emit),
        out_shape=out_shape,
        grid=(n_t,),
        in_specs=[pl.BlockSpec((rows, ssm_w), lambda i: (i, 0)),
                  pl.BlockSpec(wb.shape, lambda i: (0, 0, 0)),
                  pl.BlockSpec(a_cat.shape, lambda i: (0, 0)),
                  pl.BlockSpec(wc.shape, lambda i: (0, 0, 0)),
                  pl.BlockSpec(d_row.shape, lambda i: (0, 0)),
                  pl.BlockSpec(init.shape, lambda i: (0, 0, 0))],
        out_specs=out_specs,
        scratch_shapes=[pltpu.VMEM((2, rows, st_lanes), F32),
                        pltpu.VMEM((2, SUBLANES, st_lanes), F32)],
        compiler_params=_cparams(("arbitrary",)),
        name="ssm_scan" if emit else "ssm_state",
    )(u_rows, wb, a_cat, wc, d_row, init)


def _attn_kernel(slope_ref, qt_ref, ka_ref, vt_ref, o_ref, acc_ref, sa_ref, sb_ref):
    hp = pl.program_id(1)
    i = pl.program_id(2)
    tq = qt_ref.shape[2]
    key_row = lax.broadcasted_iota(jnp.int32, (MOBA_BLOCK, 1), 0)
    q_lane = lax.broadcasted_iota(jnp.int32, (1, tq), 1)
    causal = key_row <= q_lane
    acc_ref[...] = jnp.zeros_like(acc_ref)
    halves = KV_TILE // MOBA_BLOCK

    def scores(kt, s_ref):
        for hh in range(2):
            s_ref[hh] = _dot(ka_ref[hh, kt], qt_ref[hh])

    def col_max(s_t):
        rows = s_t.shape[0]
        while rows > SUBLANES:
            rows //= 2
            s_t = jnp.maximum(s_t[:rows], s_t[rows:])
        return jnp.max(s_t, axis=0, keepdims=True)

    def step(kt, s_ref, ms, last):
        new_ms = []
        for hh in range(2):
            slope = slope_ref[hp * 2 + hh]
            parts, shifts = [], []
            for d in range(halves):
                blk = kt * halves + d
                s_t = s_ref[hh, d * MOBA_BLOCK:(d + 1) * MOBA_BLOCK, :]
                if last:
                    s_t = jnp.where(causal | (blk < i), s_t, NEG_BIG)
                parts.append(s_t)
                shifts.append(slope * ((blk - i) * MOBA_BLOCK).astype(F32))
            m_new = ms[hh]
            for s_t, shift in zip(parts, shifts):
                m_new = jnp.maximum(m_new, col_max(s_t) + shift)
            p = jnp.concatenate([jnp.exp(s_t - (m_new - shift)) for s_t, shift in zip(parts, shifts)],
                                axis=0).astype(BF16)
            acc_ref[hh] = acc_ref[hh] * jnp.exp(ms[hh] - m_new) + _dot(vt_ref[hh, kt], p)
            new_ms.append(m_new)
        return tuple(new_ms)

    m0 = jnp.full((1, tq), 0.1 * NEG_BIG, F32)
    last_kt = i // halves

    def body(r, ms):
        scores(2 * r + 1, sb_ref)
        ms = step(2 * r, sa_ref, ms, False)
        scores(2 * r + 2, sa_ref)
        return step(2 * r + 1, sb_ref, ms, False)

    scores(0, sa_ref)
    ms = lax.fori_loop(0, last_kt // 2, body, (m0, m0))

    @pl.when(last_kt % 2 == 0)
    def _():
        step(last_kt, sa_ref, ms, True)

    @pl.when(last_kt % 2 == 1)
    def _():
        scores(last_kt, sb_ref)
        step(last_kt, sb_ref, step(last_kt - 1, sa_ref, ms, False), True)

    outs = []
    for hh in range(2):
        acc = acc_ref[hh]
        outs.append(acc[:HEAD_DIM] / acc[HEAD_DIM:HEAD_DIM + 1])
    o_ref[...] = jnp.transpose(jnp.concatenate(outs, axis=0)).astype(o_ref.dtype)


def _attention(qt, ka, vt):
    bsz, n_heads, _, s = qt.shape
    n_kt = ka.shape[2]
    n_blk = s // MOBA_BLOCK
    slopes = jnp.asarray(_alibi_slopes(n_heads), F32)
    return pl.pallas_call(
        _attn_kernel,
        out_shape=jax.ShapeDtypeStruct((bsz, s, n_heads * HEAD_DIM), BF16),
        grid=(bsz, n_heads // 2, n_blk),
        in_specs=[pl.BlockSpec(memory_space=pltpu.SMEM),
                  pl.BlockSpec((None, 2, LANES, MOBA_BLOCK), lambda b, hp, i: (b, hp, 0, i)),
                  pl.BlockSpec((None, 2, n_kt, KV_TILE, LANES), lambda b, hp, i: (b, hp, 0, 0, 0)),
                  pl.BlockSpec((None, 2, n_kt, V_ROWS, KV_TILE), lambda b, hp, i: (b, hp, 0, 0, 0))],
        out_specs=pl.BlockSpec((None, MOBA_BLOCK, LANES), lambda b, hp, i: (b, i, hp)),
        scratch_shapes=[pltpu.VMEM((2, V_ROWS, MOBA_BLOCK), F32),
                        pltpu.VMEM((2, KV_TILE, MOBA_BLOCK), F32),
                        pltpu.VMEM((2, KV_TILE, MOBA_BLOCK), F32)],
        compiler_params=_cparams(("arbitrary", "arbitrary", "arbitrary")),
        name="attn",
    )(slopes, qt, ka, vt)


def _rms(y, g):
    return y * lax.rsqrt(jnp.mean(y * y, axis=-1, keepdims=True) + NORM_EPS) * g


def _merge_kernel(x_ref, z_ref, o_ref, sg_ref, mod_ref, wa_ref, wb_ref, wo_ref, wout_ref, gpost_ref, out_ref):
    d = x_ref.shape[1]
    z = z_ref[...]
    y_a = _dot(z, wa_ref[...]) * jax.nn.sigmoid(_dot(z, wb_ref[...]))
    y_b = _dot(o_ref[...], wo_ref[...])
    merged = sg_ref[:, :d].astype(F32) * y_a + sg_ref[:, d:].astype(F32) * y_b
    y = _dot(merged.astype(BF16), wout_ref[...])
    out_ref[...] = x_ref[...] + mod_ref[2:3, :] * _rms(y, gpost_ref[...])


def _merge(x, z_tm, o, sg, mod, wa, wb, wo, wout, g_post, tm):
    bsz, s, d = x.shape
    n_t = s // tm
    half_t = n_t // 2
    ssm_w = wa.shape[0]
    full = lambda a: pl.BlockSpec(a.shape, lambda b, i: (0,) * a.ndim)
    return pl.pallas_call(
        _merge_kernel,
        out_shape=jax.ShapeDtypeStruct((bsz, s, d), F32),
        grid=(bsz, n_t),
        in_specs=[pl.BlockSpec((None, tm, d), lambda b, i: (b, i, 0)),
                  pl.BlockSpec((tm, ssm_w), lambda b, i: (i % half_t, (i // half_t) * bsz + b)),
                  pl.BlockSpec((None, tm, o.shape[2]), lambda b, i: (b, i, 0)),
                  pl.BlockSpec((None, tm, sg.shape[2]), lambda b, i: (b, i, 0)),
                  pl.BlockSpec((None, N_MOD, d), lambda b, i: (b, 0, 0)),
                  full(wa), full(wb), full(wo), full(wout), full(g_post)],
        out_specs=pl.BlockSpec((None, tm, d), lambda b, i: (b, i, 0)),
        compiler_params=_cparams(("arbitrary", "arbitrary")),
        name="merge",
    )(x, z_tm, o, sg, mod, wa, wb, wo, wout, g_post)


def _ffn_kernel(x_ref, mod_ref, gpre_ref, wg_ref, wu_ref, wd_ref, gpost_ref, out_ref, *, n_chunks):
    x = x_ref[...]
    hn = _rms(x, gpre_ref[...]) * (1.0 + mod_ref[4:5, :]) + mod_ref[3:4, :]
    hb = hn.astype(BF16)
    ff = wg_ref.shape[1]
    cw = ff // n_chunks
    y = None
    for c in range(n_chunks):
        cols = slice(c * cw, (c + 1) * cw)
        gate = _dot(hb, wg_ref[:, cols])
        up = _dot(hb, wu_ref[:, cols])
        mid = (gate * jax.nn.sigmoid(gate) * up).astype(BF16)
        part = _dot(mid, wd_ref[cols, :])
        y = part if y is None else y + part
    out_ref[...] = x + mod_ref[5:6, :] * _rms(y, gpost_ref[...])


def _ffn(x, mod, g_pre, wg, wu, wd, g_post, tm):
    bsz, s, d = x.shape
    ff = wg.shape[1]
    n_chunks = 2 if (ff // 2) % LANES == 0 else 1
    full = lambda a: pl.BlockSpec(a.shape, lambda b, i: (0,) * a.ndim)
    return pl.pallas_call(
        functools.partial(_ffn_kernel, n_chunks=n_chunks),
        out_shape=jax.ShapeDtypeStruct((bsz, s, d), F32),
        grid=(bsz, s // tm),
        in_specs=[pl.BlockSpec((None, tm, d), lambda b, i: (b, i, 0)),
                  pl.BlockSpec((None, N_MOD, d), lambda b, i: (b, 0, 0)),
                  full(g_pre), full(wg), full(wu), full(wd), full(g_post)],
        out_specs=pl.BlockSpec((None, tm, d), lambda b, i: (b, i, 0)),
        compiler_params=_cparams(("arbitrary", "arbitrary")),
        name="ffn",
    )(x, mod, g_pre, wg, wu, wd, g_post)


def _layer(x, mod, g_pre_mix, g_post_mix, w_in, ssm_a_re, ssm_a_im, ssm_log_dt, ssm_b_re, ssm_b_im,
           ssm_c_re, ssm_c_im, ssm_d, w_glu_a, w_glu_b, w_attn_out, w_out,
           g_pre_ffn, g_post_ffn, w_ff_gate, w_ff_up, w_ff_down):
    bsz, s, d = x.shape
    n_groups, n_state = ssm_a_re.shape
    ssm_w = n_groups * SSM_GROUP
    width = w_attn_out.shape[0]
    n_heads = width // HEAD_DIM
    assert 2 * bsz == SUBLANES, "SSM scan packs 2 sequence halves x batch into the 8 sublanes"
    tm = min(512, s // 2)
    steps_per_tile = min(64, s // 2)
    assert s % (2 * tm) == 0 and tm % MOBA_BLOCK == 0 and (s // 2) % steps_per_tile == 0

    mod3 = mod.reshape(bsz, N_MOD, d)
    row = lambda g: g.reshape(1, -1)

    ab_re, ab_im, bbt_re, bbt_im = _ssm_prep(ssm_a_re, ssm_a_im, ssm_log_dt, ssm_b_re, ssm_b_im)
    wb = jnp.concatenate([_block_diag_tiles(bbt_re), _block_diag_tiles(bbt_im)], axis=-1).astype(BF16)
    wc = jnp.concatenate([_block_diag_tiles(jnp.transpose(ssm_c_re, (0, 2, 1))),
                          _block_diag_tiles(-jnp.transpose(ssm_c_im, (0, 2, 1)))], axis=1).astype(BF16)
    a_cat = jnp.stack([ab_re.reshape(-1), ab_im.reshape(-1)], axis=0)
    d_row = ssm_d.reshape(1, ssm_w)

    u_tm, qt, ka, vt, sg = _proj(x, mod3, row(g_pre_mix), w_in, n_heads, ssm_w, tm)

    u_rows = u_tm.reshape((s // 2) * SUBLANES, ssm_w)
    zero_state = jnp.zeros((2, SUBLANES, n_groups * n_state), F32)
    first = _ssm_pass(u_rows, wb, a_cat, wc, d_row, zero_state, steps_per_tile, emit=False)
    init = jnp.concatenate([jnp.zeros_like(first[:, :bsz]), first[:, :bsz]], axis=1)
    z_rows, _ = _ssm_pass(u_rows, wb, a_cat, wc, d_row, init, steps_per_tile, emit=True)
    z_tm = z_rows.reshape(s // 2, SUBLANES * ssm_w)

    o = _attention(qt, ka, vt)

    x1 = _merge(x, z_tm, o, sg, mod3, w_glu_a.astype(BF16), w_glu_b.astype(BF16),
                w_attn_out.astype(BF16), w_out.astype(BF16), row(g_post_mix), tm)
    return _ffn(x1, mod3, row(g_pre_ffn), w_ff_gate.astype(BF16), w_ff_up.astype(BF16),
                w_ff_down.astype(BF16), row(g_post_ffn), tm)


def kernel(x, c, w_ada, b_ada, g_pre_mix, g_post_mix, w_in, ssm_a_re, ssm_a_im, ssm_log_dt, ssm_b_re, ssm_b_im, ssm_c_re, ssm_c_im, ssm_d, w_glu_a, w_glu_b, w_attn_out, w_out, g_pre_ffn, g_post_ffn, w_ff_gate, w_ff_up, w_ff_down):
    depth = w_ada.shape[0]
    for l in range(depth):
        mod = _adaln(c, w_ada[l], b_ada[l])
        x = _layer(x, mod, g_pre_mix[l], g_post_mix[l], w_in[l], ssm_a_re[l], ssm_a_im[l], ssm_log_dt[l],
                   ssm_b_re[l], ssm_b_im[l], ssm_c_re[l], ssm_c_im[l], ssm_d[l], w_glu_a[l], w_glu_b[l],
                   w_attn_out[l], w_out[l], g_pre_ffn[l], g_post_ffn[l], w_ff_gate[l], w_ff_up[l], w_ff_down[l])
    return x
```

```python
import functools
import math

import jax
import jax.numpy as jnp
from jax import lax
from jax.experimental import pallas as pl
from jax.experimental.pallas import tpu as pltpu

F32 = jnp.float32
BF16 = jnp.bfloat16

SSM_GROUP = 16
SSM_STATE = 64
HEAD_DIM = 64
MOBA_BLOCK = 256
MOBA_TOPK = 3
N_MOD = 6
NORM_EPS = 1e-6
LANES = 128
SUBLANES = 8
GROUPS_PER_TILE = LANES // SSM_GROUP
NEG_BIG = -1e30
MASK_FEATS = 32
V_ROWS = HEAD_DIM + 16
KV_TILE = 2 * MOBA_BLOCK
VMEM_LIMIT = 56 * 1024 * 1024


def _cparams(sem):
    return pltpu.CompilerParams(dimension_semantics=sem, vmem_limit_bytes=VMEM_LIMIT)


def _dot(a, b):
    return jnp.dot(a, b, preferred_element_type=F32)


def _dot_nt(a, b):
    return lax.dot_general(a, b, (((1,), (1,)), ((), ())), preferred_element_type=F32)


def _adaln_kernel(c_ref, w_ref, b_ref, o_ref):
    c = c_ref[...]
    act = c * jax.nn.sigmoid(c)
    o_ref[...] = jnp.dot(act, w_ref[...], preferred_element_type=F32,
                         precision=lax.Precision.HIGHEST) + b_ref[...]


def _adaln(c, w_ada, b_ada):
    bsz, d = c.shape
    n = w_ada.shape[1]
    tn = n // N_MOD
    return pl.pallas_call(
        _adaln_kernel,
        out_shape=jax.ShapeDtypeStruct((bsz, n), F32),
        grid=(n // tn,),
        in_specs=[pl.BlockSpec((bsz, d), lambda j: (0, 0)),
                  pl.BlockSpec((d, tn), lambda j: (0, j)),
                  pl.BlockSpec((1, tn), lambda j: (0, j))],
        out_specs=pl.BlockSpec((bsz, tn), lambda j: (0, j)),
        compiler_params=_cparams(("arbitrary",)),
        name="adaln",
    )(c, w_ada, b_ada.reshape(1, n))


def _ssm_prep_kernel(are_ref, aim_ref, ldt_ref, bre_ref, bim_ref,
                     abre_ref, abim_ref, bbre_ref, bbim_ref):
    a_re = are_ref[...]
    a_im = aim_ref[...]
    dt = jnp.exp(ldt_ref[...])
    mag = jnp.exp(dt * a_re)
    ab_re = mag * jnp.cos(dt * a_im)
    ab_im = mag * jnp.sin(dt * a_im)
    den = a_re * a_re + a_im * a_im
    nr = ab_re - 1.0
    co_re = (nr * a_re + ab_im * a_im) / den
    co_im = (ab_im * a_re - nr * a_im) / den
    abre_ref[...] = ab_re
    abim_ref[...] = ab_im
    b_re = bre_ref[...]
    b_im = bim_ref[...]
    bbre_ref[...] = co_re[:, None, :] * b_re - co_im[:, None, :] * b_im
    bbim_ref[...] = co_re[:, None, :] * b_im + co_im[:, None, :] * b_re


def _ssm_prep(a_re, a_im, log_dt, b_re, b_im):
    g, p = a_re.shape
    c = b_re.shape[-1]
    bt_re = jnp.transpose(b_re, (0, 2, 1))
    bt_im = jnp.transpose(b_im, (0, 2, 1))
    return pl.pallas_call(
        _ssm_prep_kernel,
        out_shape=(jax.ShapeDtypeStruct((g, p), F32), jax.ShapeDtypeStruct((g, p), F32),
                   jax.ShapeDtypeStruct((g, c, p), F32), jax.ShapeDtypeStruct((g, c, p), F32)),
        name="ssm_prep",
    )(a_re, a_im, log_dt.reshape(g, 1), bt_re, bt_im)


def _block_diag_tiles(w):
    g, r, cc = w.shape
    nt = g // GROUPS_PER_TILE
    w = w.reshape(nt, GROUPS_PER_TILE, r, cc)
    eye = jnp.eye(GROUPS_PER_TILE, dtype=w.dtype)
    return jnp.einsum('sgrc,gh->sgrhc', w, eye).reshape(nt, GROUPS_PER_TILE * r, GROUPS_PER_TILE * cc)


def _alibi_slopes(n_heads):
    return [2.0 ** (-8.0 * (h + 1) / n_heads) for h in range(n_heads)]


def _proj_kernel(x_ref, mod_ref, gpre_ref, wu_ref, wqt_ref, wk_ref, wvt_ref, wg_ref, perm_ref,
                 u_ref, qt_ref, ka_ref, vt_ref, sg_ref, kmt_ref, *, n_heads, n_blk):
    i = pl.program_id(1)
    tm = x_ref.shape[0]
    width = n_heads * HEAD_DIM
    blk_per_tile = tm // MOBA_BLOCK

    @pl.when(i == 0)
    def _():
        kmt_ref[...] = jnp.zeros_like(kmt_ref)

    x = x_ref[...]
    ms = jnp.mean(x * x, axis=-1, keepdims=True)
    hn = x * lax.rsqrt(ms + NORM_EPS) * gpre_ref[...]
    hn = hn * (1.0 + mod_ref[1:2, :]) + mod_ref[0:1, :]
    hb = hn.astype(BF16)

    u_ref[...] = _dot(hb, wu_ref[...]).astype(BF16)
    q_t = _dot_nt(wqt_ref[...], hb)
    k = _dot(hb, wk_ref[...])
    v_t = _dot_nt(wvt_ref[...], hb)
    sg_ref[...] = jax.nn.sigmoid(_dot(hb, wg_ref[...])).astype(BF16)

    head_of_lane = lax.broadcasted_iota(jnp.int32, (n_heads, width), 1) // HEAD_DIM
    head_mask = (head_of_lane == lax.broadcasted_iota(jnp.int32, (n_heads, width), 0)).astype(F32)
    for lb in range(blk_per_tile):
        km = jnp.mean(k[lb * MOBA_BLOCK:(lb + 1) * MOBA_BLOCK, :], axis=0, keepdims=True)
        row0 = pl.multiple_of((i * blk_per_tile + lb) * n_heads, n_heads)
        kmt_ref[pl.ds(row0, n_heads), :] = km * head_mask

    gate = jnp.dot(kmt_ref[...], q_t, preferred_element_type=F32, precision=lax.Precision.HIGHEST)
    gate = gate.reshape(n_blk, n_heads, tm)
    n_iota = lax.broadcasted_iota(jnp.int32, (n_blk, n_heads, tm), 0)
    own = i * blk_per_tile + lax.broadcasted_iota(jnp.int32, (n_blk, n_heads, tm), 2) // MOBA_BLOCK
    neg_inf = jnp.float32(-jnp.inf)
    cur = jnp.where(n_iota < own, gate, neg_inf)
    sel = jnp.zeros((n_blk, n_heads, tm), F32)
    for _ in range(MOBA_TOPK):
        best = jnp.max(cur, axis=0, keepdims=True)
        idx = jnp.min(jnp.where(cur == best, n_iota, n_blk), axis=0, keepdims=True)
        hit = (n_iota == idx) & (best > neg_inf)
        sel = jnp.where(hit, 1.0, sel)
        cur = jnp.where(hit, neg_inf, cur)
    unsel = 1.0 - sel - (n_iota == own).astype(F32)
    unsel = unsel.reshape(n_blk * n_heads, tm).astype(BF16)
    mask_rows = _dot(perm_ref[...], unsel) * NEG_BIG

    sub8 = lax.broadcasted_iota(jnp.int32, (SUBLANES, tm), 0)
    ones2 = (sub8 < 2).astype(F32)
    ones1 = (sub8 < 1).astype(F32)
    zeros = lambda r: jnp.zeros((r, tm), F32)
    lane = lax.broadcasted_iota(jnp.int32, (tm, LANES), 1)
    row = lax.broadcasted_iota(jnp.int32, (tm, LANES), 0)
    own_row = i * blk_per_tile + row // MOBA_BLOCK
    pos = (row % MOBA_BLOCK).astype(F32)
    scale = HEAD_DIM ** -0.5
    slopes = _alibi_slopes(n_heads)
    pad_feats = LANES - HEAD_DIM - MASK_FEATS - SUBLANES
    for h in range(n_heads):
        rows = slice(h * HEAD_DIM, (h + 1) * HEAD_DIM)
        tile = slice((h // 2) * LANES, (h // 2 + 1) * LANES)
        q_rows = q_t[rows] * scale
        m_rows = mask_rows[h * n_blk:(h + 1) * n_blk]
        if n_blk < MASK_FEATS:
            m_rows = jnp.concatenate([m_rows, zeros(MASK_FEATS - n_blk)], axis=0)
        bias = slopes[h] * pos
        bias_hi = bias.astype(BF16).astype(F32)
        bias_lo = bias - bias_hi
        if h % 2 == 0:
            qt = jnp.concatenate([q_rows, m_rows, ones2, zeros(pad_feats)], axis=0)
            data_lanes = lane < HEAD_DIM
            feat0 = HEAD_DIM
        else:
            qt = jnp.concatenate([m_rows, ones2, zeros(pad_feats), q_rows], axis=0)
            data_lanes = lane >= HEAD_DIM
            feat0 = 0
        qt_ref[h] = qt.astype(BF16)
        feats = jnp.where(lane == feat0 + MASK_FEATS, bias_hi,
                          jnp.where(lane == feat0 + MASK_FEATS + 1, bias_lo,
                                    (lane == feat0 + own_row).astype(F32)))
        ka = jnp.where(data_lanes, k[:, tile], feats).astype(BF16)
        vt = jnp.concatenate([v_t[rows], ones1, zeros(V_ROWS - HEAD_DIM - SUBLANES)], axis=0).astype(BF16)
        for kt in range(tm // KV_TILE):
            ka_ref[h, kt] = ka[kt * KV_TILE:(kt + 1) * KV_TILE]
            vt_ref[h, kt] = vt[:, kt * KV_TILE:(kt + 1) * KV_TILE]


def _row_permutation(n_blk, n_heads):
    r = jnp.arange(n_heads * n_blk)
    src = (r % n_blk) * n_heads + r // n_blk
    return (src[:, None] == jnp.arange(n_blk * n_heads)[None, :]).astype(BF16)


def _proj(x, mod, g_pre, w_in, n_heads, ssm_w, tm):
    bsz, s, d = x.shape
    n_t = s // tm
    half_t = n_t // 2
    n_blk = s // MOBA_BLOCK
    width = n_heads * HEAD_DIM
    assert n_blk <= MASK_FEATS and tm % KV_TILE == 0
    c = [0, ssm_w, ssm_w + width, ssm_w + 2 * width, ssm_w + 3 * width, w_in.shape[1]]
    gate_w = c[5] - c[4]
    wu = w_in[:, c[0]:c[1]].astype(BF16)
    wqt = w_in[:, c[1]:c[2]].T.astype(BF16)
    wk = w_in[:, c[2]:c[3]].astype(BF16)
    wvt = w_in[:, c[3]:c[4]].T.astype(BF16)
    wg = w_in[:, c[4]:c[5]].astype(BF16)
    perm = _row_permutation(n_blk, n_heads)
    full = lambda a: pl.BlockSpec(a.shape, lambda b, i: (0,) * a.ndim)
    kern = functools.partial(_proj_kernel, n_heads=n_heads, n_blk=n_blk)
    return pl.pallas_call(
        kern,
        out_shape=(jax.ShapeDtypeStruct((s // 2, 2 * bsz * ssm_w), BF16),
                   jax.ShapeDtypeStruct((bsz, n_heads, LANES, s), BF16),
                   jax.ShapeDtypeStruct((bsz, n_heads, s // KV_TILE, KV_TILE, LANES), BF16),
                   jax.ShapeDtypeStruct((bsz, n_heads, s // KV_TILE, V_ROWS, KV_TILE), BF16),
                   jax.ShapeDtypeStruct((bsz, s, gate_w), BF16)),
        grid=(bsz, n_t),
        in_specs=[pl.BlockSpec((None, tm, d), lambda b, i: (b, i, 0)),
                  pl.BlockSpec((None, N_MOD, d), lambda b, i: (b, 0, 0)),
                  full(g_pre), full(wu), full(wqt), full(wk), full(wvt), full(wg), full(perm)],
        out_specs=(pl.BlockSpec((tm, ssm_w), lambda b, i: (i % half_t, (i // half_t) * bsz + b)),
                   pl.BlockSpec((None, n_heads, LANES, tm), lambda b, i: (b, 0, 0, i)),
                   pl.BlockSpec((None, n_heads, tm // KV_TILE, KV_TILE, LANES), lambda b, i: (b, 0, i, 0, 0)),
                   pl.BlockSpec((None, n_heads, tm // KV_TILE, V_ROWS, KV_TILE), lambda b, i: (b, 0, i, 0, 0)),
                   pl.BlockSpec((None, tm, gate_w), lambda b, i: (b, i, 0))),
        scratch_shapes=[pltpu.VMEM((n_blk * n_heads, width), F32)],
        compiler_params=_cparams(("arbitrary", "arbitrary")),
        name="proj",
    )(x, mod, g_pre, wu, wqt, wk, wvt, wg, perm)


def _gelu_tanh(x):
    c = math.sqrt(2.0 / math.pi)
    return 0.5 * x * (1.0 + jnp.tanh(c * (x + 0.044715 * (x * x * x))))


def _ssm_kernel(u_ref, wb_ref, a_ref, wc_ref, d_ref, init_ref, *rest, emit):
    if emit:
        z_ref, fin_ref, bu_ref, st_ref = rest
    else:
        fin_ref, bu_ref, st_ref = rest
    i = pl.program_id(0)
    rows = u_ref.shape[0]
    steps = rows // SUBLANES
    n_tiles = wb_ref.shape[0]
    st_w = wb_ref.shape[2] // 2

    @pl.when(i == 0)
    def _():
        st_ref[...] = init_ref[...]

    for sg in range(n_tiles):
        r = _dot(u_ref[:, sg * LANES:(sg + 1) * LANES], wb_ref[sg])
        bu_ref[0, :, sg * st_w:(sg + 1) * st_w] = r[:, :st_w]
        bu_ref[1, :, sg * st_w:(sg + 1) * st_w] = r[:, st_w:]

    def step(t, carry):
        x_re, x_im = carry
        a_re = a_ref[0]
        a_im = a_ref[1]
        r0 = pl.multiple_of(t * SUBLANES, SUBLANES)
        n_re = a_re * x_re - a_im * x_im + bu_ref[0, pl.ds(r0, SUBLANES), :]
        n_im = a_re * x_im + a_im * x_re + bu_ref[1, pl.ds(r0, SUBLANES), :]
        bu_ref[0, pl.ds(r0, SUBLANES), :] = n_re
        bu_ref[1, pl.ds(r0, SUBLANES), :] = n_im
        return n_re, n_im

    x_re, x_im = lax.fori_loop(0, steps, step, (st_ref[0], st_ref[1]), unroll=2)
    st_ref[0] = x_re
    st_ref[1] = x_im
    fin_ref[0] = x_re
    fin_ref[1] = x_im

    if emit:
        for sg in range(n_tiles):
            xr = bu_ref[0, :, sg * st_w:(sg + 1) * st_w].astype(BF16)
            xi = bu_ref[1, :, sg * st_w:(sg + 1) * st_w].astype(BF16)
            y = _dot(xr, wc_ref[sg, :st_w, :]) + _dot(xi, wc_ref[sg, st_w:, :])
            lanes = slice(sg * LANES, (sg + 1) * LANES)
            y = y + d_ref[:, lanes] * u_ref[:, lanes].astype(F32)
            z_ref[:, lanes] = _gelu_tanh(y).astype(BF16)


def _ssm_pass(u_rows, wb, a_cat, wc, d_row, init, steps_per_tile, emit):
    rows_total, ssm_w = u_rows.shape
    rows = steps_per_tile * SUBLANES
    n_t = rows_total // rows
    st_lanes = a_cat.shape[2]
    fin_shape = jax.ShapeDtypeStruct((2, SUBLANES, st_lanes), F32)
    fin_spec = pl.BlockSpec((2, SUBLANES, st_lanes), lambda i: (0, 0, 0))
    if emit:
        out_shape = (jax.ShapeDtypeStruct((rows_total, ssm_w), BF16), fin_shape)
        out_specs = (pl.BlockSpec((rows, ssm_w), lambda i: (i, 0)), fin_spec)
    else:
        out_shape = fin_shape
        out_specs = fin_spec
    return pl.pallas_call(
        functools.partial(_ssm_kernel, emit=emit),
        out_shape=out_shape,
        grid=(n_t,),
        in_specs=[pl.BlockSpec((rows, ssm_w), lambda i: (i, 0)),
                  pl.BlockSpec(wb.shape, lambda i: (0, 0, 0)),
                  pl.BlockSpec(a_cat.shape, lambda i: (0, 0, 0)),
                  pl.BlockSpec(wc.shape, lambda i: (0, 0, 0)),
                  pl.BlockSpec(d_row.shape, lambda i: (0, 0)),
                  pl.BlockSpec(init.shape, lambda i: (0, 0, 0))],
        out_specs=out_specs,
        scratch_shapes=[pltpu.VMEM((2, rows, st_lanes), F32),
                        pltpu.VMEM((2, SUBLANES, st_lanes), F32)],
        compiler_params=_cparams(("arbitrary",)),
        name="ssm_scan" if emit else "ssm_state",
    )(u_rows, wb, a_cat, wc, d_row, init)


def _attn_kernel(slope_ref, qt_ref, ka_ref, vt_ref, o_ref, acc_ref, sa_ref, sb_ref):
    hp = pl.program_id(1)
    i = pl.program_id(2)
    tq = qt_ref.shape[2]
    key_row = lax.broadcasted_iota(jnp.int32, (KV_TILE, 1), 0)
    q_lane = lax.broadcasted_iota(jnp.int32, (1, tq), 1)
    causal = key_row <= q_lane
    acc_ref[...] = jnp.zeros_like(acc_ref)
    halves = KV_TILE // MOBA_BLOCK

    def scores(kt, s_ref):
        for hh in range(2):
            s_ref[hh] = _dot(ka_ref[hh, kt], qt_ref[hh])

    def col_max(s_t):
        rows = s_t.shape[0]
        while rows > SUBLANES:
            rows //= 2
            s_t = jnp.maximum(s_t[:rows], s_t[rows:])
        return jnp.max(s_t, axis=0, keepdims=True)

    def step(kt, s_ref, ms, last):
        new_ms = []
        for hh in range(2):
            slope = slope_ref[hp * 2 + hh]
            parts, shifts = [], []
            for d in range(halves):
                rows = slice(d * MOBA_BLOCK, (d + 1) * MOBA_BLOCK)
                s_t = s_ref[hh, rows, :]
                if last:
                    s_t = jnp.where(causal[rows], s_t, NEG_BIG)
                parts.append(s_t)
                shifts.append(slope * (((kt - i) * halves + d) * MOBA_BLOCK).astype(F32))
            m_new = ms[hh]
            for s_t, shift in zip(parts, shifts):
                m_new = jnp.maximum(m_new, col_max(s_t) + shift)
            p = jnp.concatenate([jnp.exp(s_t - (m_new - shift)) for s_t, shift in zip(parts, shifts)],
                                axis=0).astype(BF16)
            acc_ref[hh] = acc_ref[hh] * jnp.exp(ms[hh] - m_new) + _dot(vt_ref[hh, kt], p)
            new_ms.append(m_new)
        return tuple(new_ms)

    m0 = jnp.full((1, tq), 0.1 * NEG_BIG, F32)
    last_kt = i

    def body(r, ms):
        scores(2 * r + 1, sb_ref)
        ms = step(2 * r, sa_ref, ms, False)
        scores(2 * r + 2, sa_ref)
        return step(2 * r + 1, sb_ref, ms, False)

    scores(0, sa_ref)
    ms = lax.fori_loop(0, last_kt // 2, body, (m0, m0))

    @pl.when(last_kt % 2 == 0)
    def _():
        step(last_kt, sa_ref, ms, True)

    @pl.when(last_kt % 2 == 1)
    def _():
        scores(last_kt, sb_ref)
        step(last_kt, sb_ref, step(last_kt - 1, sa_ref, ms, False), True)

    outs = []
    for hh in range(2):
        acc = acc_ref[hh]
        outs.append(acc[:HEAD_DIM] / acc[HEAD_DIM:HEAD_DIM + 1])
    o_ref[...] = jnp.transpose(jnp.concatenate(outs, axis=0)).astype(o_ref.dtype)


def _attention(qt, ka, vt):
    bsz, n_heads, _, s = qt.shape
    n_kt = ka.shape[2]
    tq = KV_TILE
    slopes = jnp.asarray(_alibi_slopes(n_heads), F32)
    return pl.pallas_call(
        _attn_kernel,
        out_shape=jax.ShapeDtypeStruct((bsz, s, n_heads * HEAD_DIM), BF16),
        grid=(bsz, n_heads // 2, n_kt),
        in_specs=[pl.BlockSpec(memory_space=pltpu.SMEM),
                  pl.BlockSpec((None, 2, LANES, tq), lambda b, hp, i: (b, hp, 0, i)),
                  pl.BlockSpec((None, 2, n_kt, KV_TILE, LANES), lambda b, hp, i: (b, hp, 0, 0, 0)),
                  pl.BlockSpec((None, 2, n_kt, V_ROWS, KV_TILE), lambda b, hp, i: (b, hp, 0, 0, 0))],
        out_specs=pl.BlockSpec((None, tq, LANES), lambda b, hp, i: (b, i, hp)),
        scratch_shapes=[pltpu.VMEM((2, V_ROWS, tq), F32),
                        pltpu.VMEM((2, KV_TILE, tq), F32),
                        pltpu.VMEM((2, KV_TILE, tq), F32)],
        compiler_params=_cparams(("arbitrary", "arbitrary", "arbitrary")),
        name="attn",
    )(slopes, qt, ka, vt)


def _rms(y, g):
    return y * lax.rsqrt(jnp.mean(y * y, axis=-1, keepdims=True) + NORM_EPS) * g


def _merge_kernel(x_ref, z_ref, o_ref, sg_ref, mod_ref, wa_ref, wb_ref, wo_ref, wout_ref, gpost_ref, out_ref):
    d = x_ref.shape[1]
    z = z_ref[...]
    y_a = _dot(z, wa_ref[...]) * jax.nn.sigmoid(_dot(z, wb_ref[...]))
    y_b = _dot(o_ref[...], wo_ref[...])
    merged = sg_ref[:, :d].astype(F32) * y_a + sg_ref[:, d:].astype(F32) * y_b
    y = _dot(merged.astype(BF16), wout_ref[...])
    out_ref[...] = x_ref[...] + mod_ref[2:3, :] * _rms(y, gpost_ref[...])


def _merge(x, z_tm, o, sg, mod, wa, wb, wo, wout, g_post, tm):
    bsz, s, d = x.shape
    n_t = s // tm
    half_t = n_t // 2
    ssm_w = wa.shape[0]
    full = lambda a: pl.BlockSpec(a.shape, lambda b, i: (0,) * a.ndim)
    return pl.pallas_call(
        _merge_kernel,
        out_shape=jax.ShapeDtypeStruct((bsz, s, d), F32),
        grid=(bsz, n_t),
        in_specs=[pl.BlockSpec((None, tm, d), lambda b, i: (b, i, 0)),
                  pl.BlockSpec((tm, ssm_w), lambda b, i: (i % half_t, (i // half_t) * bsz + b)),
                  pl.BlockSpec((None, tm, o.shape[2]), lambda b, i: (b, i, 0)),
                  pl.BlockSpec((None, tm, sg.shape[2]), lambda b, i: (b, i, 0)),
                  pl.BlockSpec((None, N_MOD, d), lambda b, i: (b, 0, 0)),
                  full(wa), full(wb), full(wo), full(wout), full(g_post)],
        out_specs=pl.BlockSpec((None, tm, d), lambda b, i: (b, i, 0)),
        compiler_params=_cparams(("arbitrary", "arbitrary")),
        name="merge",
    )(x, z_tm, o, sg, mod, wa, wb, wo, wout, g_post)


def _ffn_kernel(x_ref, mod_ref, gpre_ref, wg_ref, wu_ref, wd_ref, gpost_ref, out_ref, *, n_chunks):
    x = x_ref[...]
    hn = _rms(x, gpre_ref[...]) * (1.0 + mod_ref[4:5, :]) + mod_ref[3:4, :]
    hb = hn.astype(BF16)
    ff = wg_ref.shape[1]
    cw = ff // n_chunks
    y = None
    for c in range(n_chunks):
        cols = slice(c * cw, (c + 1) * cw)
        gate = _dot(hb, wg_ref[:, cols])
        up = _dot(hb, wu_ref[:, cols])
        mid = (gate * jax.nn.sigmoid(gate) * up).astype(BF16)
        part = _dot(mid, wd_ref[cols, :])
        y = part if y is None else y + part
    out_ref[...] = x + mod_ref[5:6, :] * _rms(y, gpost_ref[...])


def _ffn(x, mod, g_pre, wg, wu, wd, g_post, tm):
    bsz, s, d = x.shape
    ff = wg.shape[1]
    n_chunks = 2 if (ff // 2) % LANES == 0 else 1
    full = lambda a: pl.BlockSpec(a.shape, lambda b, i: (0,) * a.ndim)
    return pl.pallas_call(
        functools.partial(_ffn_kernel, n_chunks=n_chunks),
        out_shape=jax.ShapeDtypeStruct((bsz, s, d), F32),
        grid=(bsz, s // tm),
        in_specs=[pl.BlockSpec((None, tm, d), lambda b, i: (b, i, 0)),
                  pl.BlockSpec((None, N_MOD, d), lambda b, i: (b, 0, 0)),
                  full(g_pre), full(wg), full(wu), full(wd), full(g_post)],
        out_specs=pl.BlockSpec((None, tm, d), lambda b, i: (b, i, 0)),
        compiler_params=_cparams(("arbitrary", "arbitrary")),
        name="ffn",
    )(x, mod, g_pre, wg, wu, wd, g_post)


def _layer(x, mod, g_pre_mix, g_post_mix, w_in, ssm_a_re, ssm_a_im, ssm_log_dt, ssm_b_re, ssm_b_im,
           ssm_c_re, ssm_c_im, ssm_d, w_glu_a, w_glu_b, w_attn_out, w_out,
           g_pre_ffn, g_post_ffn, w_ff_gate, w_ff_up, w_ff_down):
    bsz, s, d = x.shape
    n_groups, n_state = ssm_a_re.shape
    ssm_w = n_groups * SSM_GROUP
    width = w_attn_out.shape[0]
    n_heads = width // HEAD_DIM
    assert 2 * bsz == SUBLANES, "SSM scan packs 2 sequence halves x batch into the 8 sublanes"
    tm = min(512, s // 2)
    steps_per_tile = min(64, s // 2)
    assert s % (2 * tm) == 0 and tm % MOBA_BLOCK == 0 and (s // 2) % steps_per_tile == 0

    mod3 = mod.reshape(bsz, N_MOD, d)
    row = lambda g: g.reshape(1, -1)

    ab_re, ab_im, bbt_re, bbt_im = _ssm_prep(ssm_a_re, ssm_a_im, ssm_log_dt, ssm_b_re, ssm_b_im)
    wb = jnp.concatenate([_block_diag_tiles(bbt_re), _block_diag_tiles(bbt_im)], axis=-1).astype(BF16)
    wc = jnp.concatenate([_block_diag_tiles(jnp.transpose(ssm_c_re, (0, 2, 1))),
                          _block_diag_tiles(-jnp.transpose(ssm_c_im, (0, 2, 1)))], axis=1).astype(BF16)
    a_cat = jnp.stack([ab_re.reshape(-1), ab_im.reshape(-1)], axis=0)
    a_cat = jnp.broadcast_to(a_cat[:, None, :], (2, SUBLANES, a_cat.shape[1]))
    d_row = ssm_d.reshape(1, ssm_w)

    u_tm, qt, ka, vt, sg = _proj(x, mod3, row(g_pre_mix), w_in, n_heads, ssm_w, tm)

    u_rows = u_tm.reshape((s // 2) * SUBLANES, ssm_w)
    zero_state = jnp.zeros((2, SUBLANES, n_groups * n_state), F32)
    first = _ssm_pass(u_rows, wb, a_cat, wc, d_row, zero_state, steps_per_tile, emit=False)
    init = jnp.concatenate([jnp.zeros_like(first[:, :bsz]), first[:, :bsz]], axis=1)
    z_rows, _ = _ssm_pass(u_rows, wb, a_cat, wc, d_row, init, steps_per_tile, emit=True)
    z_tm = z_rows.reshape(s // 2, SUBLANES * ssm_w)

    o = _attention(qt, ka, vt)

    x1 = _merge(x, z_tm, o, sg, mod3, w_glu_a.astype(BF16), w_glu_b.astype(BF16),
                w_attn_out.astype(BF16), w_out.astype(BF16), row(g_post_mix), tm)
    return _ffn(x1, mod3, row(g_pre_ffn), w_ff_gate.astype(BF16), w_ff_up.astype(BF16),
                w_ff_down.astype(BF16), row(g_post_ffn), tm)


def kernel(x, c, w_ada, b_ada, g_pre_mix, g_post_mix, w_in, ssm_a_re, ssm_a_im, ssm_log_dt, ssm_b_re, ssm_b_im, ssm_c_re, ssm_c_im, ssm_d, w_glu_a, w_glu_b, w_attn_out, w_out, g_pre_ffn, g_post_ffn, w_ff_gate, w_ff_up, w_ff_down):
    depth = w_ada.shape[0]
    for l in range(depth):
        mod = _adaln(c, w_ada[l], b_ada[l])
        x = _layer(x, mod, g_pre_mix[l], g_post_mix[l], w_in[l], ssm_a_re[l], ssm_a_im[l], ssm_log_dt[l],
                   ssm_b_re[l], ssm_b_im[l], ssm_c_re[l], ssm_c_im[l], ssm_d[l], w_glu_a[l], w_glu_b[l],
                   w_attn_out[l], w_out[l], g_pre_ffn[l], g_post_ffn[l], w_ff_gate[l], w_ff_up[l], w_ff_down[l])
    return x
```

```python
import functools
import math

import jax
import jax.numpy as jnp
from jax import lax
from jax.experimental import pallas as pl
from jax.experimental.pallas import tpu as pltpu

F32 = jnp.float32
BF16 = jnp.bfloat16

SSM_GROUP = 16
SSM_STATE = 64
HEAD_DIM = 64
MOBA_BLOCK = 256
MOBA_TOPK = 3
N_MOD = 6
NORM_EPS = 1e-6
LANES = 128
SUBLANES = 8
GROUPS_PER_TILE = LANES // SSM_GROUP
NEG_BIG = -1e30
LOG2E = math.log2(math.e)
MASK_FEATS = 32
V_ROWS = HEAD_DIM + 16
KV_TILE = 2 * MOBA_BLOCK
VMEM_LIMIT = 56 * 1024 * 1024


def _cparams(sem):
    return pltpu.CompilerParams(dimension_semantics=sem, vmem_limit_bytes=VMEM_LIMIT)


def _dot(a, b):
    return jnp.dot(a, b, preferred_element_type=F32)


def _dot_nt(a, b):
    return lax.dot_general(a, b, (((1,), (1,)), ((), ())), preferred_element_type=F32)


def _adaln_kernel(c_ref, w_ref, b_ref, o_ref):
    c = c_ref[...]
    act = c * jax.nn.sigmoid(c)
    o_ref[...] = jnp.dot(act, w_ref[...], preferred_element_type=F32,
                         precision=lax.Precision.HIGHEST) + b_ref[...]


def _adaln(c, w_ada, b_ada):
    bsz, d = c.shape
    n = w_ada.shape[1]
    tn = n // N_MOD
    return pl.pallas_call(
        _adaln_kernel,
        out_shape=jax.ShapeDtypeStruct((bsz, n), F32),
        grid=(n // tn,),
        in_specs=[pl.BlockSpec((bsz, d), lambda j: (0, 0)),
                  pl.BlockSpec((d, tn), lambda j: (0, j)),
                  pl.BlockSpec((1, tn), lambda j: (0, j))],
        out_specs=pl.BlockSpec((bsz, tn), lambda j: (0, j)),
        compiler_params=_cparams(("arbitrary",)),
        name="adaln",
    )(c, w_ada, b_ada.reshape(1, n))


def _ssm_prep_kernel(are_ref, aim_ref, ldt_ref, bre_ref, bim_ref,
                     abre_ref, abim_ref, bbre_ref, bbim_ref):
    a_re = are_ref[...]
    a_im = aim_ref[...]
    dt = jnp.exp(ldt_ref[...])
    mag = jnp.exp(dt * a_re)
    ab_re = mag * jnp.cos(dt * a_im)
    ab_im = mag * jnp.sin(dt * a_im)
    den = a_re * a_re + a_im * a_im
    nr = ab_re - 1.0
    co_re = (nr * a_re + ab_im * a_im) / den
    co_im = (ab_im * a_re - nr * a_im) / den
    abre_ref[...] = ab_re
    abim_ref[...] = ab_im
    b_re = bre_ref[...]
    b_im = bim_ref[...]
    bbre_ref[...] = co_re[:, None, :] * b_re - co_im[:, None, :] * b_im
    bbim_ref[...] = co_re[:, None, :] * b_im + co_im[:, None, :] * b_re


def _ssm_prep(a_re, a_im, log_dt, b_re, b_im):
    g, p = a_re.shape
    c = b_re.shape[-1]
    bt_re = jnp.transpose(b_re, (0, 2, 1))
    bt_im = jnp.transpose(b_im, (0, 2, 1))
    return pl.pallas_call(
        _ssm_prep_kernel,
        out_shape=(jax.ShapeDtypeStruct((g, p), F32), jax.ShapeDtypeStruct((g, p), F32),
                   jax.ShapeDtypeStruct((g, c, p), F32), jax.ShapeDtypeStruct((g, c, p), F32)),
        name="ssm_prep",
    )(a_re, a_im, log_dt.reshape(g, 1), bt_re, bt_im)


def _block_diag_tiles(w):
    g, r, cc = w.shape
    nt = g // GROUPS_PER_TILE
    w = w.reshape(nt, GROUPS_PER_TILE, r, cc)
    eye = jnp.eye(GROUPS_PER_TILE, dtype=w.dtype)
    return jnp.einsum('sgrc,gh->sgrhc', w, eye).reshape(nt, GROUPS_PER_TILE * r, GROUPS_PER_TILE * cc)


def _alibi_slopes(n_heads):
    return [2.0 ** (-8.0 * (h + 1) / n_heads) for h in range(n_heads)]


def _proj_kernel(x_ref, mod_ref, gpre_ref, wu_ref, wqt_ref, wk_ref, wvt_ref, wg_ref, perm_ref,
                 u_ref, qt_ref, ka_ref, vt_ref, sg_ref, kmt_ref, *, n_heads, n_blk):
    i = pl.program_id(1)
    tm = x_ref.shape[0]
    width = n_heads * HEAD_DIM
    blk_per_tile = tm // MOBA_BLOCK

    @pl.when(i == 0)
    def _():
        kmt_ref[...] = jnp.zeros_like(kmt_ref)

    x = x_ref[...]
    ms = jnp.mean(x * x, axis=-1, keepdims=True)
    hn = x * lax.rsqrt(ms + NORM_EPS) * gpre_ref[...]
    hn = hn * (1.0 + mod_ref[1:2, :]) + mod_ref[0:1, :]
    hb = hn.astype(BF16)

    u_ref[...] = _dot(hb, wu_ref[...]).astype(BF16)
    q_t = _dot_nt(wqt_ref[...], hb)
    k = _dot(hb, wk_ref[...])
    v_t = _dot_nt(wvt_ref[...], hb)
    sg_ref[...] = jax.nn.sigmoid(_dot(hb, wg_ref[...])).astype(BF16)

    head_of_lane = lax.broadcasted_iota(jnp.int32, (n_heads, width), 1) // HEAD_DIM
    head_mask = (head_of_lane == lax.broadcasted_iota(jnp.int32, (n_heads, width), 0)).astype(F32)
    for lb in range(blk_per_tile):
        km = jnp.mean(k[lb * MOBA_BLOCK:(lb + 1) * MOBA_BLOCK, :], axis=0, keepdims=True)
        row0 = pl.multiple_of((i * blk_per_tile + lb) * n_heads, n_heads)
        kmt_ref[pl.ds(row0, n_heads), :] = km * head_mask

    gate = jnp.dot(kmt_ref[...], q_t, preferred_element_type=F32, precision=lax.Precision.HIGHEST)
    gate = gate.reshape(n_blk, n_heads, tm)
    n_iota = lax.broadcasted_iota(jnp.int32, (n_blk, n_heads, tm), 0)
    own = i * blk_per_tile + lax.broadcasted_iota(jnp.int32, (n_blk, n_heads, tm), 2) // MOBA_BLOCK
    neg_inf = jnp.float32(-jnp.inf)
    cur = jnp.where(n_iota < own, gate, neg_inf)
    sel = jnp.zeros((n_blk, n_heads, tm), F32)
    for _ in range(MOBA_TOPK):
        best = jnp.max(cur, axis=0, keepdims=True)
        idx = jnp.min(jnp.where(cur == best, n_iota, n_blk), axis=0, keepdims=True)
        hit = (n_iota == idx) & (best > neg_inf)
        sel = jnp.where(hit, 1.0, sel)
        cur = jnp.where(hit, neg_inf, cur)
    unsel = 1.0 - sel - (n_iota == own).astype(F32)
    unsel = unsel.reshape(n_blk * n_heads, tm).astype(BF16)
    mask_rows = _dot(perm_ref[...], unsel) * NEG_BIG

    sub8 = lax.broadcasted_iota(jnp.int32, (SUBLANES, tm), 0)
    ones3 = (sub8 < 3).astype(F32)
    ones1 = (sub8 < 1).astype(F32)
    zeros = lambda r: jnp.zeros((r, tm), F32)
    lane = lax.broadcasted_iota(jnp.int32, (tm, LANES), 1)
    row = lax.broadcasted_iota(jnp.int32, (tm, LANES), 0)
    own_row = i * blk_per_tile + row // MOBA_BLOCK
    pos = (row % MOBA_BLOCK).astype(F32)
    scale = HEAD_DIM ** -0.5 * LOG2E
    slopes = _alibi_slopes(n_heads)
    pad_feats = LANES - HEAD_DIM - MASK_FEATS - SUBLANES
    for h in range(n_heads):
        rows = slice(h * HEAD_DIM, (h + 1) * HEAD_DIM)
        tile = slice((h // 2) * LANES, (h // 2 + 1) * LANES)
        q_rows = q_t[rows] * scale
        m_rows = mask_rows[h * n_blk:(h + 1) * n_blk]
        if n_blk < MASK_FEATS:
            m_rows = jnp.concatenate([m_rows, zeros(MASK_FEATS - n_blk)], axis=0)
        bias = (slopes[h] * LOG2E) * pos
        bias_hi = bias.astype(BF16).astype(F32)
        bias_mid = (bias - bias_hi).astype(BF16).astype(F32)
        bias_lo = bias - bias_hi - bias_mid
        if h % 2 == 0:
            qt = jnp.concatenate([q_rows, m_rows, ones3, zeros(pad_feats)], axis=0)
            data_lanes = lane < HEAD_DIM
            feat0 = HEAD_DIM
        else:
            qt = jnp.concatenate([m_rows, ones3, zeros(pad_feats), q_rows], axis=0)
            data_lanes = lane >= HEAD_DIM
            feat0 = 0
        qt_ref[h] = qt.astype(BF16)
        feats = jnp.where(lane == feat0 + MASK_FEATS, bias_hi,
                          jnp.where(lane == feat0 + MASK_FEATS + 1, bias_mid,
                                    jnp.where(lane == feat0 + MASK_FEATS + 2, bias_lo,
                                              (lane == feat0 + own_row).astype(F32))))
        ka = jnp.where(data_lanes, k[:, tile], feats).astype(BF16)
        vt = jnp.concatenate([v_t[rows], ones1, zeros(V_ROWS - HEAD_DIM - SUBLANES)], axis=0).astype(BF16)
        for kt in range(tm // KV_TILE):
            ka_ref[h, kt] = ka[kt * KV_TILE:(kt + 1) * KV_TILE]
            vt_ref[h, kt] = vt[:, kt * KV_TILE:(kt + 1) * KV_TILE]


def _row_permutation(n_blk, n_heads):
    r = jnp.arange(n_heads * n_blk)
    src = (r % n_blk) * n_heads + r // n_blk
    return (src[:, None] == jnp.arange(n_blk * n_heads)[None, :]).astype(BF16)


def _proj(x, mod, g_pre, w_in, n_heads, ssm_w, tm):
    bsz, s, d = x.shape
    n_t = s // tm
    half_t = n_t // 2
    n_blk = s // MOBA_BLOCK
    width = n_heads * HEAD_DIM
    assert n_blk <= MASK_FEATS and tm % KV_TILE == 0
    c = [0, ssm_w, ssm_w + width, ssm_w + 2 * width, ssm_w + 3 * width, w_in.shape[1]]
    gate_w = c[5] - c[4]
    wu = w_in[:, c[0]:c[1]].astype(BF16)
    wqt = w_in[:, c[1]:c[2]].T.astype(BF16)
    wk = w_in[:, c[2]:c[3]].astype(BF16)
    wvt = w_in[:, c[3]:c[4]].T.astype(BF16)
    wg = w_in[:, c[4]:c[5]].astype(BF16)
    perm = _row_permutation(n_blk, n_heads)
    full = lambda a: pl.BlockSpec(a.shape, lambda b, i: (0,) * a.ndim)
    kern = functools.partial(_proj_kernel, n_heads=n_heads, n_blk=n_blk)
    return pl.pallas_call(
        kern,
        out_shape=(jax.ShapeDtypeStruct((s // 2, 2 * bsz * ssm_w), BF16),
                   jax.ShapeDtypeStruct((bsz, n_heads, LANES, s), BF16),
                   jax.ShapeDtypeStruct((bsz, n_heads, s // KV_TILE, KV_TILE, LANES), BF16),
                   jax.ShapeDtypeStruct((bsz, n_heads, s // KV_TILE, V_ROWS, KV_TILE), BF16),
                   jax.ShapeDtypeStruct((bsz, s, gate_w), BF16)),
        grid=(bsz, n_t),
        in_specs=[pl.BlockSpec((None, tm, d), lambda b, i: (b, i, 0)),
                  pl.BlockSpec((None, N_MOD, d), lambda b, i: (b, 0, 0)),
                  full(g_pre), full(wu), full(wqt), full(wk), full(wvt), full(wg), full(perm)],
        out_specs=(pl.BlockSpec((tm, ssm_w), lambda b, i: (i % half_t, (i // half_t) * bsz + b)),
                   pl.BlockSpec((None, n_heads, LANES, tm), lambda b, i: (b, 0, 0, i)),
                   pl.BlockSpec((None, n_heads, tm // KV_TILE, KV_TILE, LANES), lambda b, i: (b, 0, i, 0, 0)),
                   pl.BlockSpec((None, n_heads, tm // KV_TILE, V_ROWS, KV_TILE), lambda b, i: (b, 0, i, 0, 0)),
                   pl.BlockSpec((None, tm, gate_w), lambda b, i: (b, i, 0))),
        scratch_shapes=[pltpu.VMEM((n_blk * n_heads, width), F32)],
        compiler_params=_cparams(("arbitrary", "arbitrary")),
        name="proj",
    )(x, mod, g_pre, wu, wqt, wk, wvt, wg, perm)


def _gelu_tanh(x):
    c = math.sqrt(2.0 / math.pi)
    return 0.5 * x * (1.0 + jnp.tanh(c * (x + 0.044715 * (x * x * x))))


def _ssm_kernel(u_ref, perm_ref, permt_ref, wb_ref, a_ref, wc_ref, d_ref, init_ref, *rest, emit):
    if emit:
        z_ref, fin_ref, bu_ref, st_ref, us_ref, zs_ref = rest
    else:
        fin_ref, bu_ref, st_ref, us_ref = rest
    i = pl.program_id(0)
    steps = u_ref.shape[0]
    ssm_w = u_ref.shape[1] // SUBLANES
    n_tiles = wb_ref.shape[0]
    st_w = wb_ref.shape[2] // 2

    @pl.when(i == 0)
    def _():
        st_ref[...] = init_ref[...]

    u_nat = jnp.concatenate([u_ref[:, s * ssm_w:(s + 1) * ssm_w] for s in range(SUBLANES)], axis=0)
    us_ref[...] = _dot(perm_ref[...], u_nat).astype(BF16)

    for sg in range(n_tiles):
        r = _dot(us_ref[:, sg * LANES:(sg + 1) * LANES], wb_ref[sg])
        bu_ref[0, :, sg * st_w:(sg + 1) * st_w] = r[:, :st_w]
        bu_ref[1, :, sg * st_w:(sg + 1) * st_w] = r[:, st_w:]

    def step(t, carry):
        x_re, x_im = carry
        a_re = a_ref[0]
        a_im = a_ref[1]
        r0 = pl.multiple_of(t * SUBLANES, SUBLANES)
        n_re = a_re * x_re - a_im * x_im + bu_ref[0, pl.ds(r0, SUBLANES), :]
        n_im = a_re * x_im + a_im * x_re + bu_ref[1, pl.ds(r0, SUBLANES), :]
        bu_ref[0, pl.ds(r0, SUBLANES), :] = n_re
        bu_ref[1, pl.ds(r0, SUBLANES), :] = n_im
        return n_re, n_im

    x_re, x_im = lax.fori_loop(0, steps, step, (st_ref[0], st_ref[1]), unroll=2)
    st_ref[0] = x_re
    st_ref[1] = x_im
    fin_ref[0] = x_re
    fin_ref[1] = x_im

    if emit:
        for sg in range(n_tiles):
            xr = bu_ref[0, :, sg * st_w:(sg + 1) * st_w].astype(BF16)
            xi = bu_ref[1, :, sg * st_w:(sg + 1) * st_w].astype(BF16)
            y = _dot(xr, wc_ref[sg, :st_w, :]) + _dot(xi, wc_ref[sg, st_w:, :])
            lanes = slice(sg * LANES, (sg + 1) * LANES)
            y = y + d_ref[:, lanes] * us_ref[:, lanes].astype(F32)
            zs_ref[:, lanes] = _gelu_tanh(y).astype(BF16)
        z_nat = _dot(permt_ref[...], zs_ref[...]).astype(BF16)
        for s in range(SUBLANES):
            z_ref[:, s * ssm_w:(s + 1) * ssm_w] = z_nat[s * steps:(s + 1) * steps]


def _ssm_pass(u_tm, wb, a_cat, wc, d_row, init, steps_per_tile, emit):
    n_steps, slot_w = u_tm.shape
    ssm_w = slot_w // SUBLANES
    rows = steps_per_tile * SUBLANES
    n_t = n_steps // steps_per_tile
    st_lanes = a_cat.shape[2]
    r = jnp.arange(rows)
    src = (r % SUBLANES) * steps_per_tile + r // SUBLANES
    perm = (src[:, None] == r[None, :]).astype(BF16)
    perm_t = perm.T
    fin_shape = jax.ShapeDtypeStruct((2, SUBLANES, st_lanes), F32)
    fin_spec = pl.BlockSpec((2, SUBLANES, st_lanes), lambda i: (0, 0, 0))
    scratch = [pltpu.VMEM((2, rows, st_lanes), F32), pltpu.VMEM((2, SUBLANES, st_lanes), F32),
               pltpu.VMEM((rows, ssm_w), BF16)]
    if emit:
        out_shape = (jax.ShapeDtypeStruct((n_steps, slot_w), BF16), fin_shape)
        out_specs = (pl.BlockSpec((steps_per_tile, slot_w), lambda i: (i, 0)), fin_spec)
        scratch.append(pltpu.VMEM((rows, ssm_w), BF16))
    else:
        out_shape = fin_shape
        out_specs = fin_spec
    return pl.pallas_call(
        functools.partial(_ssm_kernel, emit=emit),
        out_shape=out_shape,
        grid=(n_t,),
        in_specs=[pl.BlockSpec((steps_per_tile, slot_w), lambda i: (i, 0)),
                  pl.BlockSpec(perm.shape, lambda i: (0, 0)),
                  pl.BlockSpec(perm.shape, lambda i: (0, 0)),
                  pl.BlockSpec(wb.shape, lambda i: (0, 0, 0)),
                  pl.BlockSpec(a_cat.shape, lambda i: (0, 0, 0)),
                  pl.BlockSpec(wc.shape, lambda i: (0, 0, 0)),
                  pl.BlockSpec(d_row.shape, lambda i: (0, 0)),
                  pl.BlockSpec(init.shape, lambda i: (0, 0, 0))],
        out_specs=out_specs,
        scratch_shapes=scratch,
        compiler_params=_cparams(("arbitrary",)),
        name="ssm_scan" if emit else "ssm_state",
    )(u_tm, perm, perm_t, wb, a_cat, wc, d_row, init)


def _attn_kernel(slope_ref, qt_ref, ka_ref, vt_ref, o_ref, acc_ref, sa_ref, sb_ref):
    hp = pl.program_id(1)
    i = pl.program_id(2)
    tq = qt_ref.shape[2]
    key_row = lax.broadcasted_iota(jnp.int32, (KV_TILE, 1), 0)
    q_lane = lax.broadcasted_iota(jnp.int32, (1, tq), 1)
    causal = key_row <= q_lane
    acc_ref[...] = jnp.zeros_like(acc_ref)
    halves = KV_TILE // MOBA_BLOCK

    def scores(kt, s_ref):
        for hh in range(2):
            s_ref[hh] = _dot(ka_ref[hh, kt], qt_ref[hh])

    def col_max(s_t):
        rows = s_t.shape[0]
        while rows > SUBLANES:
            rows //= 2
            s_t = jnp.maximum(s_t[:rows], s_t[rows:])
        return jnp.max(s_t, axis=0, keepdims=True)

    def step(kt, s_ref, ms, last):
        new_ms = []
        for hh in range(2):
            slope = slope_ref[hp * 2 + hh]
            parts, shifts = [], []
            for d in range(halves):
                rows = slice(d * MOBA_BLOCK, (d + 1) * MOBA_BLOCK)
                s_t = s_ref[hh, rows, :]
                if last:
                    s_t = jnp.where(causal[rows], s_t, NEG_BIG)
                parts.append(s_t)
                shifts.append(slope * (((kt - i) * halves + d) * MOBA_BLOCK).astype(F32))
            m_new = ms[hh]
            for s_t, shift in zip(parts, shifts):
                m_new = jnp.maximum(m_new, col_max(s_t) + shift)
            p = jnp.concatenate([jnp.exp2(s_t - (m_new - shift)) for s_t, shift in zip(parts, shifts)],
                                axis=0).astype(BF16)
            acc_ref[hh] = acc_ref[hh] * jnp.exp2(ms[hh] - m_new) + _dot(vt_ref[hh, kt], p)
            new_ms.append(m_new)
        return tuple(new_ms)

    m0 = jnp.full((1, tq), 0.1 * NEG_BIG, F32)
    last_kt = i

    def body(r, ms):
        scores(2 * r + 1, sb_ref)
        ms = step(2 * r, sa_ref, ms, False)
        scores(2 * r + 2, sa_ref)
        return step(2 * r + 1, sb_ref, ms, False)

    scores(0, sa_ref)
    ms = lax.fori_loop(0, last_kt // 2, body, (m0, m0))

    @pl.when(last_kt % 2 == 0)
    def _():
        step(last_kt, sa_ref, ms, True)

    @pl.when(last_kt % 2 == 1)
    def _():
        scores(last_kt, sb_ref)
        step(last_kt, sb_ref, step(last_kt - 1, sa_ref, ms, False), True)


    outs = []
    for hh in range(2):
        acc = acc_ref[hh]
        outs.append(acc[:HEAD_DIM] / acc[HEAD_DIM:HEAD_DIM + 1])
    o_ref[...] = jnp.transpose(jnp.concatenate(outs, axis=0)).astype(o_ref.dtype)


def _attention(qt, ka, vt):
    bsz, n_heads, _, s = qt.shape
    n_kt = ka.shape[2]
    tq = KV_TILE
    slopes = jnp.asarray([sl * LOG2E for sl in _alibi_slopes(n_heads)], F32)
    return pl.pallas_call(
        _attn_kernel,
        out_shape=jax.ShapeDtypeStruct((bsz, s, n_heads * HEAD_DIM), BF16),
        grid=(bsz, n_heads // 2, n_kt),
        in_specs=[pl.BlockSpec(memory_space=pltpu.SMEM),
                  pl.BlockSpec((None, 2, LANES, tq), lambda b, hp, i: (b, hp, 0, i)),
                  pl.BlockSpec((None, 2, n_kt, KV_TILE, LANES), lambda b, hp, i: (b, hp, 0, 0, 0)),
                  pl.BlockSpec((None, 2, n_kt, V_ROWS, KV_TILE), lambda b, hp, i: (b, hp, 0, 0, 0))],
        out_specs=pl.BlockSpec((None, tq, LANES), lambda b, hp, i: (b, i, hp)),
        scratch_shapes=[pltpu.VMEM((2, V_ROWS, tq), F32),
                        pltpu.VMEM((2, KV_TILE, tq), F32),
                        pltpu.VMEM((2, KV_TILE, tq), F32)],
        compiler_params=_cparams(("arbitrary", "arbitrary", "arbitrary")),
        name="attn",
    )(slopes, qt, ka, vt)


def _rms(y, g):
    return y * lax.rsqrt(jnp.mean(y * y, axis=-1, keepdims=True) + NORM_EPS) * g


def _merge_kernel(x_ref, z_ref, o_ref, sg_ref, mod_ref, wa_ref, wb_ref, wo_ref, wout_ref, gpost_ref, out_ref):
    d = x_ref.shape[1]
    z = z_ref[...]
    y_a = _dot(z, wa_ref[...]) * jax.nn.sigmoid(_dot(z, wb_ref[...]))
    y_b = _dot(o_ref[...], wo_ref[...])
    merged = sg_ref[:, :d].astype(F32) * y_a + sg_ref[:, d:].astype(F32) * y_b
    y = _dot(merged.astype(BF16), wout_ref[...])
    out_ref[...] = x_ref[...] + mod_ref[2:3, :] * _rms(y, gpost_ref[...])


def _merge(x, z_tm, o, sg, mod, wa, wb, wo, wout, g_post, tm):
    bsz, s, d = x.shape
    n_t = s // tm
    half_t = n_t // 2
    ssm_w = wa.shape[0]
    full = lambda a: pl.BlockSpec(a.shape, lambda b, i: (0,) * a.ndim)
    return pl.pallas_call(
        _merge_kernel,
        out_shape=jax.ShapeDtypeStruct((bsz, s, d), F32),
        grid=(bsz, n_t),
        in_specs=[pl.BlockSpec((None, tm, d), lambda b, i: (b, i, 0)),
                  pl.BlockSpec((tm, ssm_w), lambda b, i: (i % half_t, (i // half_t) * bsz + b)),
                  pl.BlockSpec((None, tm, o.shape[2]), lambda b, i: (b, i, 0)),
                  pl.BlockSpec((None, tm, sg.shape[2]), lambda b, i: (b, i, 0)),
                  pl.BlockSpec((None, N_MOD, d), lambda b, i: (b, 0, 0)),
                  full(wa), full(wb), full(wo), full(wout), full(g_post)],
        out_specs=pl.BlockSpec((None, tm, d), lambda b, i: (b, i, 0)),
        compiler_params=_cparams(("arbitrary", "arbitrary")),
        name="merge",
    )(x, z_tm, o, sg, mod, wa, wb, wo, wout, g_post)


def _ffn_kernel(x_ref, mod_ref, gpre_ref, wg_ref, wu_ref, wd_ref, gpost_ref, out_ref, *, n_chunks):
    x = x_ref[...]
    hn = _rms(x, gpre_ref[...]) * (1.0 + mod_ref[4:5, :]) + mod_ref[3:4, :]
    hb = hn.astype(BF16)
    ff = wg_ref.shape[1]
    cw = ff // n_chunks
    y = None
    for c in range(n_chunks):
        cols = slice(c * cw, (c + 1) * cw)
        gate = _dot(hb, wg_ref[:, cols])
        up = _dot(hb, wu_ref[:, cols])
        mid = (gate * jax.nn.sigmoid(gate) * up).astype(BF16)
        part = _dot(mid, wd_ref[cols, :])
        y = part if y is None else y + part
    out_ref[...] = x + mod_ref[5:6, :] * _rms(y, gpost_ref[...])


def _ffn(x, mod, g_pre, wg, wu, wd, g_post, tm):
    bsz, s, d = x.shape
    ff = wg.shape[1]
    n_chunks = 2 if (ff // 2) % LANES == 0 else 1
    full = lambda a: pl.BlockSpec(a.shape, lambda b, i: (0,) * a.ndim)
    return pl.pallas_call(
        functools.partial(_ffn_kernel, n_chunks=n_chunks),
        out_shape=jax.ShapeDtypeStruct((bsz, s, d), F32),
        grid=(bsz, s // tm),
        in_specs=[pl.BlockSpec((None, tm, d), lambda b, i: (b, i, 0)),
                  pl.BlockSpec((None, N_MOD, d), lambda b, i: (b, 0, 0)),
                  full(g_pre), full(wg), full(wu), full(wd), full(g_post)],
        out_specs=pl.BlockSpec((None, tm, d), lambda b, i: (b, i, 0)),
        compiler_params=_cparams(("arbitrary", "arbitrary")),
        name="ffn",
    )(x, mod, g_pre, wg, wu, wd, g_post)


def _layer(x, mod, g_pre_mix, g_post_mix, w_in, ssm_a_re, ssm_a_im, ssm_log_dt, ssm_b_re, ssm_b_im,
           ssm_c_re, ssm_c_im, ssm_d, w_glu_a, w_glu_b, w_attn_out, w_out,
           g_pre_ffn, g_post_ffn, w_ff_gate, w_ff_up, w_ff_down):
    bsz, s, d = x.shape
    n_groups, n_state = ssm_a_re.shape
    ssm_w = n_groups * SSM_GROUP
    width = w_attn_out.shape[0]
    n_heads = width // HEAD_DIM
    assert 2 * bsz == SUBLANES, "SSM scan packs 2 sequence halves x batch into the 8 sublanes"
    tm = min(512, s // 2)
    steps_per_tile = min(64, s // 2)
    assert s % (2 * tm) == 0 and tm % MOBA_BLOCK == 0 and (s // 2) % steps_per_tile == 0

    mod3 = mod.reshape(bsz, N_MOD, d)
    row = lambda g: g.reshape(1, -1)

    ab_re, ab_im, bbt_re, bbt_im = _ssm_prep(ssm_a_re, ssm_a_im, ssm_log_dt, ssm_b_re, ssm_b_im)
    wb = jnp.concatenate([_block_diag_tiles(bbt_re), _block_diag_tiles(bbt_im)], axis=-1).astype(BF16)
    wc = jnp.concatenate([_block_diag_tiles(jnp.transpose(ssm_c_re, (0, 2, 1))),
                          _block_diag_tiles(-jnp.transpose(ssm_c_im, (0, 2, 1)))], axis=1).astype(BF16)
    a_cat = jnp.stack([ab_re.reshape(-1), ab_im.reshape(-1)], axis=0)
    a_cat = jnp.broadcast_to(a_cat[:, None, :], (2, SUBLANES, a_cat.shape[1]))
    d_row = ssm_d.reshape(1, ssm_w)

    u_tm, qt, ka, vt, sg = _proj(x, mod3, row(g_pre_mix), w_in, n_heads, ssm_w, tm)

    zero_state = jnp.zeros((2, SUBLANES, n_groups * n_state), F32)
    first = _ssm_pass(u_tm, wb, a_cat, wc, d_row, zero_state, steps_per_tile, emit=False)
    init = jnp.concatenate([jnp.zeros_like(first[:, :bsz]), first[:, :bsz]], axis=1)
    z_tm, _ = _ssm_pass(u_tm, wb, a_cat, wc, d_row, init, steps_per_tile, emit=True)

    o = _attention(qt, ka, vt)

    x1 = _merge(x, z_tm, o, sg, mod3, w_glu_a.astype(BF16), w_glu_b.astype(BF16),
                w_attn_out.astype(BF16), w_out.astype(BF16), row(g_post_mix), tm)
    return _ffn(x1, mod3, row(g_pre_ffn), w_ff_gate.astype(BF16), w_ff_up.astype(BF16),
                w_ff_down.astype(BF16), row(g_post_ffn), tm)


def kernel(x, c, w_ada, b_ada, g_pre_mix, g_post_mix, w_in, ssm_a_re, ssm_a_im, ssm_log_dt, ssm_b_re, ssm_b_im, ssm_c_re, ssm_c_im, ssm_d, w_glu_a, w_glu_b, w_attn_out, w_out, g_pre_ffn, g_post_ffn, w_ff_gate, w_ff_up, w_ff_down):
    depth = w_ada.shape[0]
    for l in range(depth):
        mod = _adaln(c, w_ada[l], b_ada[l])
        x = _layer(x, mod, g_pre_mix[l], g_post_mix[l], w_in[l], ssm_a_re[l], ssm_a_im[l], ssm_log_dt[l],
                   ssm_b_re[l], ssm_b_im[l], ssm_c_re[l], ssm_c_im[l], ssm_d[l], w_glu_a[l], w_glu_b[l],
                   w_attn_out[l], w_out[l], g_pre_ffn[l], g_post_ffn[l], w_ff_gate[l], w_ff_up[l], w_ff_down[l])
    return x
```

```python
import functools
import math

import jax
import jax.numpy as jnp
from jax import lax
from jax.experimental import pallas as pl
from jax.experimental.pallas import tpu as pltpu

F32 = jnp.float32
BF16 = jnp.bfloat16

SSM_GROUP = 16
SSM_STATE = 64
HEAD_DIM = 64
MOBA_BLOCK = 256
MOBA_TOPK = 3
N_MOD = 6
NORM_EPS = 1e-6
LANES = 128
SUBLANES = 8
GROUPS_PER_TILE = LANES // SSM_GROUP
NEG_BIG = -1e30
LOG2E = math.log2(math.e)
MASK_FEATS = 32
V_ROWS = HEAD_DIM + 16
KV_TILE = 2 * MOBA_BLOCK
VMEM_LIMIT = 56 * 1024 * 1024


def _cparams(sem):
    return pltpu.CompilerParams(dimension_semantics=sem, vmem_limit_bytes=VMEM_LIMIT)


def _dot(a, b):
    return jnp.dot(a, b, preferred_element_type=F32)


def _dot_nt(a, b):
    return lax.dot_general(a, b, (((1,), (1,)), ((), ())), preferred_element_type=F32)


def _adaln_kernel(c_ref, w_ref, b_ref, o_ref):
    c = c_ref[...]
    act = c * jax.nn.sigmoid(c)
    o_ref[...] = jnp.dot(act, w_ref[...], preferred_element_type=F32,
                         precision=lax.Precision.HIGHEST) + b_ref[...]


def _adaln(c, w_ada, b_ada):
    bsz, d = c.shape
    n = w_ada.shape[1]
    tn = n // N_MOD
    return pl.pallas_call(
        _adaln_kernel,
        out_shape=jax.ShapeDtypeStruct((bsz, n), F32),
        grid=(n // tn,),
        in_specs=[pl.BlockSpec((bsz, d), lambda j: (0, 0)),
                  pl.BlockSpec((d, tn), lambda j: (0, j)),
                  pl.BlockSpec((1, tn), lambda j: (0, j))],
        out_specs=pl.BlockSpec((bsz, tn), lambda j: (0, j)),
        compiler_params=_cparams(("arbitrary",)),
        name="adaln",
    )(c, w_ada, b_ada.reshape(1, n))


def _ssm_prep_kernel(are_ref, aim_ref, ldt_ref, bre_ref, bim_ref,
                     abre_ref, abim_ref, bbre_ref, bbim_ref):
    a_re = are_ref[...]
    a_im = aim_ref[...]
    dt = jnp.exp(ldt_ref[...])
    mag = jnp.exp(dt * a_re)
    ab_re = mag * jnp.cos(dt * a_im)
    ab_im = mag * jnp.sin(dt * a_im)
    den = a_re * a_re + a_im * a_im
    nr = ab_re - 1.0
    co_re = (nr * a_re + ab_im * a_im) / den
    co_im = (ab_im * a_re - nr * a_im) / den
    abre_ref[...] = ab_re
    abim_ref[...] = ab_im
    b_re = bre_ref[...]
    b_im = bim_ref[...]
    bbre_ref[...] = co_re[:, None, :] * b_re - co_im[:, None, :] * b_im
    bbim_ref[...] = co_re[:, None, :] * b_im + co_im[:, None, :] * b_re


def _ssm_prep(a_re, a_im, log_dt, b_re, b_im):
    g, p = a_re.shape
    c = b_re.shape[-1]
    bt_re = jnp.transpose(b_re, (0, 2, 1))
    bt_im = jnp.transpose(b_im, (0, 2, 1))
    return pl.pallas_call(
        _ssm_prep_kernel,
        out_shape=(jax.ShapeDtypeStruct((g, p), F32), jax.ShapeDtypeStruct((g, p), F32),
                   jax.ShapeDtypeStruct((g, c, p), F32), jax.ShapeDtypeStruct((g, c, p), F32)),
        name="ssm_prep",
    )(a_re, a_im, log_dt.reshape(g, 1), bt_re, bt_im)


def _block_diag_tiles(w):
    g, r, cc = w.shape
    nt = g // GROUPS_PER_TILE
    w = w.reshape(nt, GROUPS_PER_TILE, r, cc)
    eye = jnp.eye(GROUPS_PER_TILE, dtype=w.dtype)
    return jnp.einsum('sgrc,gh->sgrhc', w, eye).reshape(nt, GROUPS_PER_TILE * r, GROUPS_PER_TILE * cc)


def _alibi_slopes(n_heads):
    return [2.0 ** (-8.0 * (h + 1) / n_heads) for h in range(n_heads)]


def _proj_kernel(x_ref, mod_ref, gpre_ref, wu_ref, wqt_ref, wk_ref, wvt_ref, wg_ref, perm_ref,
                 u_ref, qt_ref, ka_ref, vt_ref, sg_ref, kmt_ref, *, n_heads, n_blk):
    i = pl.program_id(1)
    tm = x_ref.shape[0]
    width = n_heads * HEAD_DIM
    blk_per_tile = tm // MOBA_BLOCK

    @pl.when(i == 0)
    def _():
        kmt_ref[...] = jnp.zeros_like(kmt_ref)

    x = x_ref[...]
    ms = jnp.mean(x * x, axis=-1, keepdims=True)
    hn = x * lax.rsqrt(ms + NORM_EPS) * gpre_ref[...]
    hn = hn * (1.0 + mod_ref[1:2, :]) + mod_ref[0:1, :]
    hb = hn.astype(BF16)

    u_ref[...] = _dot(hb, wu_ref[...]).astype(BF16)
    q_t = _dot_nt(wqt_ref[...], hb)
    k = _dot(hb, wk_ref[...])
    v_t = _dot_nt(wvt_ref[...], hb)
    sg_ref[...] = jax.nn.sigmoid(_dot(hb, wg_ref[...])).astype(BF16)

    head_of_lane = lax.broadcasted_iota(jnp.int32, (n_heads, width), 1) // HEAD_DIM
    head_mask = (head_of_lane == lax.broadcasted_iota(jnp.int32, (n_heads, width), 0)).astype(F32)
    for lb in range(blk_per_tile):
        km = jnp.mean(k[lb * MOBA_BLOCK:(lb + 1) * MOBA_BLOCK, :], axis=0, keepdims=True)
        row0 = pl.multiple_of((i * blk_per_tile + lb) * n_heads, n_heads)
        kmt_ref[pl.ds(row0, n_heads), :] = km * head_mask

    gate = jnp.dot(kmt_ref[...], q_t, preferred_element_type=F32, precision=lax.Precision.HIGHEST)
    gate = gate.reshape(n_blk, n_heads, tm)
    n_iota = lax.broadcasted_iota(jnp.int32, (n_blk, n_heads, tm), 0)
    own = i * blk_per_tile + lax.broadcasted_iota(jnp.int32, (n_blk, n_heads, tm), 2) // MOBA_BLOCK
    neg_inf = jnp.float32(-jnp.inf)
    cur = jnp.where(n_iota < own, gate, neg_inf)
    sel = jnp.zeros((n_blk, n_heads, tm), F32)
    for _ in range(MOBA_TOPK):
        best = jnp.max(cur, axis=0, keepdims=True)
        idx = jnp.min(jnp.where(cur == best, n_iota, n_blk), axis=0, keepdims=True)
        hit = (n_iota == idx) & (best > neg_inf)
        sel = jnp.where(hit, 1.0, sel)
        cur = jnp.where(hit, neg_inf, cur)
    unsel = 1.0 - sel - (n_iota == own).astype(F32)
    unsel = unsel.reshape(n_blk * n_heads, tm).astype(BF16)
    mask_rows = _dot(perm_ref[...], unsel) * NEG_BIG

    sub8 = lax.broadcasted_iota(jnp.int32, (SUBLANES, tm), 0)
    ones3 = (sub8 < 3).astype(F32)
    ones1 = (sub8 < 1).astype(F32)
    zeros = lambda r: jnp.zeros((r, tm), F32)
    lane = lax.broadcasted_iota(jnp.int32, (tm, LANES), 1)
    row = lax.broadcasted_iota(jnp.int32, (tm, LANES), 0)
    own_row = i * blk_per_tile + row // MOBA_BLOCK
    pos = (row % MOBA_BLOCK).astype(F32)
    scale = HEAD_DIM ** -0.5 * LOG2E
    slopes = _alibi_slopes(n_heads)
    pad_feats = LANES - HEAD_DIM - MASK_FEATS - SUBLANES
    for h in range(n_heads):
        rows = slice(h * HEAD_DIM, (h + 1) * HEAD_DIM)
        tile = slice((h // 2) * LANES, (h // 2 + 1) * LANES)
        q_rows = q_t[rows] * scale
        m_rows = mask_rows[h * n_blk:(h + 1) * n_blk]
        if n_blk < MASK_FEATS:
            m_rows = jnp.concatenate([m_rows, zeros(MASK_FEATS - n_blk)], axis=0)
        bias = (slopes[h] * LOG2E) * pos
        bias_hi = bias.astype(BF16).astype(F32)
        bias_mid = (bias - bias_hi).astype(BF16).astype(F32)
        bias_lo = bias - bias_hi - bias_mid
        if h % 2 == 0:
            qt = jnp.concatenate([q_rows, m_rows, ones3, zeros(pad_feats)], axis=0)
            data_lanes = lane < HEAD_DIM
            feat0 = HEAD_DIM
        else:
            qt = jnp.concatenate([m_rows, ones3, zeros(pad_feats), q_rows], axis=0)
            data_lanes = lane >= HEAD_DIM
            feat0 = 0
        qt_ref[h] = qt.astype(BF16)
        feats = jnp.where(lane == feat0 + MASK_FEATS, bias_hi,
                          jnp.where(lane == feat0 + MASK_FEATS + 1, bias_mid,
                                    jnp.where(lane == feat0 + MASK_FEATS + 2, bias_lo,
                                              (lane == feat0 + own_row).astype(F32))))
        ka = jnp.where(data_lanes, k[:, tile], feats).astype(BF16)
        vt = jnp.concatenate([v_t[rows], ones1, zeros(V_ROWS - HEAD_DIM - SUBLANES)], axis=0).astype(BF16)
        for kt in range(tm // KV_TILE):
            ka_ref[h, kt] = ka[kt * KV_TILE:(kt + 1) * KV_TILE]
            vt_ref[h, kt] = vt[:, kt * KV_TILE:(kt + 1) * KV_TILE]


def _row_permutation(n_blk, n_heads):
    r = jnp.arange(n_heads * n_blk)
    src = (r % n_blk) * n_heads + r // n_blk
    return (src[:, None] == jnp.arange(n_blk * n_heads)[None, :]).astype(BF16)


def _proj(x, mod, g_pre, w_in, n_heads, ssm_w, tm):
    bsz, s, d = x.shape
    n_t = s // tm
    half_t = n_t // 2
    n_blk = s // MOBA_BLOCK
    width = n_heads * HEAD_DIM
    assert n_blk <= MASK_FEATS and tm % KV_TILE == 0
    c = [0, ssm_w, ssm_w + width, ssm_w + 2 * width, ssm_w + 3 * width, w_in.shape[1]]
    gate_w = c[5] - c[4]
    wu = w_in[:, c[0]:c[1]].astype(BF16)
    wqt = w_in[:, c[1]:c[2]].T.astype(BF16)
    wk = w_in[:, c[2]:c[3]].astype(BF16)
    wvt = w_in[:, c[3]:c[4]].T.astype(BF16)
    wg = w_in[:, c[4]:c[5]].astype(BF16)
    perm = _row_permutation(n_blk, n_heads)
    full = lambda a: pl.BlockSpec(a.shape, lambda b, i: (0,) * a.ndim)
    kern = functools.partial(_proj_kernel, n_heads=n_heads, n_blk=n_blk)
    return pl.pallas_call(
        kern,
        out_shape=(jax.ShapeDtypeStruct((s // 2, 2 * bsz * ssm_w), BF16),
                   jax.ShapeDtypeStruct((bsz, n_heads, LANES, s), BF16),
                   jax.ShapeDtypeStruct((bsz, n_heads, s // KV_TILE, KV_TILE, LANES), BF16),
                   jax.ShapeDtypeStruct((bsz, n_heads, s // KV_TILE, V_ROWS, KV_TILE), BF16),
                   jax.ShapeDtypeStruct((bsz, s, gate_w), BF16)),
        grid=(bsz, n_t),
        in_specs=[pl.BlockSpec((None, tm, d), lambda b, i: (b, i, 0)),
                  pl.BlockSpec((None, N_MOD, d), lambda b, i: (b, 0, 0)),
                  full(g_pre), full(wu), full(wqt), full(wk), full(wvt), full(wg), full(perm)],
        out_specs=(pl.BlockSpec((tm, ssm_w), lambda b, i: (i % half_t, (i // half_t) * bsz + b)),
                   pl.BlockSpec((None, n_heads, LANES, tm), lambda b, i: (b, 0, 0, i)),
                   pl.BlockSpec((None, n_heads, tm // KV_TILE, KV_TILE, LANES), lambda b, i: (b, 0, i, 0, 0)),
                   pl.BlockSpec((None, n_heads, tm // KV_TILE, V_ROWS, KV_TILE), lambda b, i: (b, 0, i, 0, 0)),
                   pl.BlockSpec((None, tm, gate_w), lambda b, i: (b, i, 0))),
        scratch_shapes=[pltpu.VMEM((n_blk * n_heads, width), F32)],
        compiler_params=_cparams(("arbitrary", "arbitrary")),
        name="proj",
    )(x, mod, g_pre, wu, wqt, wk, wvt, wg, perm)


def _gelu_tanh(x):
    c = math.sqrt(2.0 / math.pi)
    return 0.5 * x * (1.0 + jnp.tanh(c * (x + 0.044715 * (x * x * x))))


def _ssm_kernel(u_ref, un_ref, perm_ref, permt_ref, wb_ref, a_ref, wc_ref, d_ref, init_ref, *rest, emit):
    if emit:
        z_ref, fin_ref, bua_ref, bub_ref, usa_ref, usb_ref, st_ref, zs_ref = rest
    else:
        fin_ref, bua_ref, bub_ref, usa_ref, usb_ref, st_ref = rest
    i = pl.program_id(0)
    steps = un_ref.shape[0]
    ssm_w = u_ref.shape[1] // SUBLANES
    n_tiles = wb_ref.shape[0]
    st_w = wb_ref.shape[2] // 2

    def project(src_ref, t0, bu_ref, us_ref):
        u_nat = jnp.concatenate([src_ref[t0:t0 + steps, s * ssm_w:(s + 1) * ssm_w]
                                 for s in range(SUBLANES)], axis=0)
        us_ref[...] = _dot(perm_ref[...], u_nat).astype(BF16)
        for sg in range(n_tiles):
            r = _dot(us_ref[:, sg * LANES:(sg + 1) * LANES], wb_ref[sg])
            bu_ref[0, :, sg * st_w:(sg + 1) * st_w] = r[:, :st_w]
            bu_ref[1, :, sg * st_w:(sg + 1) * st_w] = r[:, st_w:]

    def scan(bu_ref, x_re, x_im):
        for t in range(steps):
            rows = slice(t * SUBLANES, (t + 1) * SUBLANES)
            a_re = a_ref[0]
            a_im = a_ref[1]
            n_re = a_re * x_re - a_im * x_im + bu_ref[0, rows, :]
            n_im = a_re * x_im + a_im * x_re + bu_ref[1, rows, :]
            bu_ref[0, rows, :] = n_re
            bu_ref[1, rows, :] = n_im
            x_re, x_im = n_re, n_im
        return x_re, x_im

    def output(bu_ref, us_ref, t0):
        for sg in range(n_tiles):
            xr = bu_ref[0, :, sg * st_w:(sg + 1) * st_w].astype(BF16)
            xi = bu_ref[1, :, sg * st_w:(sg + 1) * st_w].astype(BF16)
            y = _dot(xr, wc_ref[sg, :st_w, :]) + _dot(xi, wc_ref[sg, st_w:, :])
            lanes = slice(sg * LANES, (sg + 1) * LANES)
            y = y + d_ref[:, lanes] * us_ref[:, lanes].astype(F32)
            zs_ref[:, lanes] = _gelu_tanh(y).astype(BF16)
        z_nat = _dot(permt_ref[...], zs_ref[...]).astype(BF16)
        for s in range(SUBLANES):
            z_ref[t0:t0 + steps, s * ssm_w:(s + 1) * ssm_w] = z_nat[s * steps:(s + 1) * steps]

    @pl.when(i == 0)
    def _():
        st_ref[...] = init_ref[...]
        project(u_ref, 0, bua_ref, usa_ref)

    project(u_ref, steps, bub_ref, usb_ref)
    x_re, x_im = scan(bua_ref, st_ref[0], st_ref[1])
    if emit:
        output(bua_ref, usa_ref, 0)
    project(un_ref, 0, bua_ref, usa_ref)
    x_re, x_im = scan(bub_ref, x_re, x_im)
    if emit:
        output(bub_ref, usb_ref, steps)
    st_ref[0] = x_re
    st_ref[1] = x_im
    fin_ref[0] = x_re
    fin_ref[1] = x_im


def _ssm_pass(u_tm, wb, a_cat, wc, d_row, init, steps_per_tile, emit):
    n_steps, slot_w = u_tm.shape
    ssm_w = slot_w // SUBLANES
    rows = steps_per_tile * SUBLANES
    n_tiles = n_steps // steps_per_tile
    assert n_tiles % 2 == 0
    st_lanes = a_cat.shape[2]
    r = jnp.arange(rows)
    src = (r % SUBLANES) * steps_per_tile + r // SUBLANES
    perm = (src[:, None] == r[None, :]).astype(BF16)
    perm_t = perm.T
    fin_shape = jax.ShapeDtypeStruct((2, SUBLANES, st_lanes), F32)
    fin_spec = pl.BlockSpec((2, SUBLANES, st_lanes), lambda i: (0, 0, 0))
    scratch = [pltpu.VMEM((2, rows, st_lanes), F32), pltpu.VMEM((2, rows, st_lanes), F32),
               pltpu.VMEM((rows, ssm_w), BF16), pltpu.VMEM((rows, ssm_w), BF16),
               pltpu.VMEM((2, SUBLANES, st_lanes), F32)]
    if emit:
        out_shape = (jax.ShapeDtypeStruct((n_steps, slot_w), BF16), fin_shape)
        out_specs = (pl.BlockSpec((2 * steps_per_tile, slot_w), lambda i: (i, 0)), fin_spec)
        scratch.append(pltpu.VMEM((rows, ssm_w), BF16))
    else:
        out_shape = fin_shape
        out_specs = fin_spec
    return pl.pallas_call(
        functools.partial(_ssm_kernel, emit=emit),
        out_shape=out_shape,
        grid=(n_tiles // 2,),
        in_specs=[pl.BlockSpec((2 * steps_per_tile, slot_w), lambda i: (i, 0)),
                  pl.BlockSpec((steps_per_tile, slot_w), lambda i: (jnp.minimum(2 * i + 2, n_tiles - 1), 0)),
                  pl.BlockSpec(perm.shape, lambda i: (0, 0)),
                  pl.BlockSpec(perm.shape, lambda i: (0, 0)),
                  pl.BlockSpec(wb.shape, lambda i: (0, 0, 0)),
                  pl.BlockSpec(a_cat.shape, lambda i: (0, 0, 0)),
                  pl.BlockSpec(wc.shape, lambda i: (0, 0, 0)),
                  pl.BlockSpec(d_row.shape, lambda i: (0, 0)),
                  pl.BlockSpec(init.shape, lambda i: (0, 0, 0))],
        out_specs=out_specs,
        scratch_shapes=scratch,
        compiler_params=_cparams(("arbitrary",)),
        name="ssm_scan" if emit else "ssm_state",
    )(u_tm, u_tm, perm, perm_t, wb, a_cat, wc, d_row, init)


def _attn_kernel(slope_ref, qt_ref, ka_ref, vt_ref, o_ref, acc_ref, sa_ref, sb_ref):
    hp = pl.program_id(1)
    i = pl.program_id(2)
    tq = qt_ref.shape[2]
    key_row = lax.broadcasted_iota(jnp.int32, (KV_TILE, 1), 0)
    q_lane = lax.broadcasted_iota(jnp.int32, (1, tq), 1)
    causal = key_row <= q_lane
    acc_ref[...] = jnp.zeros_like(acc_ref)
    halves = KV_TILE // MOBA_BLOCK

    def scores(kt, s_ref):
        for hh in range(2):
            s_ref[hh] = _dot(ka_ref[hh, kt], qt_ref[hh])

    def col_max(s_t):
        rows = s_t.shape[0]
        while rows > SUBLANES:
            rows //= 2
            s_t = jnp.maximum(s_t[:rows], s_t[rows:])
        return jnp.max(s_t, axis=0, keepdims=True)

    def step(kt, s_ref, ms, last):
        new_ms = []
        for hh in range(2):
            slope = slope_ref[hp * 2 + hh]
            parts, shifts = [], []
            for d in range(halves):
                rows = slice(d * MOBA_BLOCK, (d + 1) * MOBA_BLOCK)
                s_t = s_ref[hh, rows, :]
                if last:
                    s_t = jnp.where(causal[rows], s_t, NEG_BIG)
                parts.append(s_t)
                shifts.append(slope * (((kt - i) * halves + d) * MOBA_BLOCK).astype(F32))
            m_new = ms[hh]
            for s_t, shift in zip(parts, shifts):
                m_new = jnp.maximum(m_new, col_max(s_t) + shift)
            p = jnp.concatenate([jnp.exp2(s_t - (m_new - shift)) for s_t, shift in zip(parts, shifts)],
                                axis=0).astype(BF16)
            acc_ref[hh] = acc_ref[hh] * jnp.exp2(ms[hh] - m_new) + _dot(vt_ref[hh, kt], p)
            new_ms.append(m_new)
        return tuple(new_ms)

    m0 = jnp.full((1, tq), 0.1 * NEG_BIG, F32)
    last_kt = i

    def body(r, ms):
        scores(2 * r + 1, sb_ref)
        ms = step(2 * r, sa_ref, ms, False)
        scores(2 * r + 2, sa_ref)
        return step(2 * r + 1, sb_ref, ms, False)

    scores(0, sa_ref)
    ms = lax.fori_loop(0, last_kt // 2, body, (m0, m0))

    @pl.when(last_kt % 2 == 0)
    def _():
        step(last_kt, sa_ref, ms, True)

    @pl.when(last_kt % 2 == 1)
    def _():
        scores(last_kt, sb_ref)
        step(last_kt, sb_ref, step(last_kt - 1, sa_ref, ms, False), True)


    outs = []
    for hh in range(2):
        acc = acc_ref[hh]
        outs.append(acc[:HEAD_DIM] / acc[HEAD_DIM:HEAD_DIM + 1])
    o_ref[...] = jnp.transpose(jnp.concatenate(outs, axis=0)).astype(o_ref.dtype)


def _attention(qt, ka, vt):
    bsz, n_heads, _, s = qt.shape
    n_kt = ka.shape[2]
    tq = KV_TILE
    slopes = jnp.asarray([sl * LOG2E for sl in _alibi_slopes(n_heads)], F32)
    return pl.pallas_call(
        _attn_kernel,
        out_shape=jax.ShapeDtypeStruct((bsz, s, n_heads * HEAD_DIM), BF16),
        grid=(bsz, n_heads // 2, n_kt),
        in_specs=[pl.BlockSpec(memory_space=pltpu.SMEM),
                  pl.BlockSpec((None, 2, LANES, tq), lambda b, hp, i: (b, hp, 0, i)),
                  pl.BlockSpec((None, 2, n_kt, KV_TILE, LANES), lambda b, hp, i: (b, hp, 0, 0, 0)),
                  pl.BlockSpec((None, 2, n_kt, V_ROWS, KV_TILE), lambda b, hp, i: (b, hp, 0, 0, 0))],
        out_specs=pl.BlockSpec((None, tq, LANES), lambda b, hp, i: (b, i, hp)),
        scratch_shapes=[pltpu.VMEM((2, V_ROWS, tq), F32),
                        pltpu.VMEM((2, KV_TILE, tq), F32),
                        pltpu.VMEM((2, KV_TILE, tq), F32)],
        compiler_params=_cparams(("arbitrary", "arbitrary", "arbitrary")),
        name="attn",
    )(slopes, qt, ka, vt)


def _rms(y, g):
    return y * lax.rsqrt(jnp.mean(y * y, axis=-1, keepdims=True) + NORM_EPS) * g


def _merge_kernel(x_ref, z_ref, o_ref, sg_ref, mod_ref, wa_ref, wb_ref, wo_ref, wout_ref, gpost_ref, out_ref):
    d = x_ref.shape[1]
    z = z_ref[...]
    y_a = _dot(z, wa_ref[...]) * jax.nn.sigmoid(_dot(z, wb_ref[...]))
    y_b = _dot(o_ref[...], wo_ref[...])
    merged = sg_ref[:, :d].astype(F32) * y_a + sg_ref[:, d:].astype(F32) * y_b
    y = _dot(merged.astype(BF16), wout_ref[...])
    out_ref[...] = x_ref[...] + mod_ref[2:3, :] * _rms(y, gpost_ref[...])


def _merge(x, z_tm, o, sg, mod, wa, wb, wo, wout, g_post, tm):
    bsz, s, d = x.shape
    n_t = s // tm
    half_t = n_t // 2
    ssm_w = wa.shape[0]
    full = lambda a: pl.BlockSpec(a.shape, lambda b, i: (0,) * a.ndim)
    return pl.pallas_call(
        _merge_kernel,
        out_shape=jax.ShapeDtypeStruct((bsz, s, d), F32),
        grid=(bsz, n_t),
        in_specs=[pl.BlockSpec((None, tm, d), lambda b, i: (b, i, 0)),
                  pl.BlockSpec((tm, ssm_w), lambda b, i: (i % half_t, (i // half_t) * bsz + b)),
                  pl.BlockSpec((None, tm, o.shape[2]), lambda b, i: (b, i, 0)),
                  pl.BlockSpec((None, tm, sg.shape[2]), lambda b, i: (b, i, 0)),
                  pl.BlockSpec((None, N_MOD, d), lambda b, i: (b, 0, 0)),
                  full(wa), full(wb), full(wo), full(wout), full(g_post)],
        out_specs=pl.BlockSpec((None, tm, d), lambda b, i: (b, i, 0)),
        compiler_params=_cparams(("arbitrary", "arbitrary")),
        name="merge",
    )(x, z_tm, o, sg, mod, wa, wb, wo, wout, g_post)


def _ffn_kernel(x_ref, mod_ref, gpre_ref, wg_ref, wu_ref, wd_ref, gpost_ref, out_ref, *, n_chunks):
    x = x_ref[...]
    hn = _rms(x, gpre_ref[...]) * (1.0 + mod_ref[4:5, :]) + mod_ref[3:4, :]
    hb = hn.astype(BF16)
    ff = wg_ref.shape[1]
    cw = ff // n_chunks
    y = None
    for c in range(n_chunks):
        cols = slice(c * cw, (c + 1) * cw)
        gate = _dot(hb, wg_ref[:, cols])
        up = _dot(hb, wu_ref[:, cols])
        mid = (gate * jax.nn.sigmoid(gate) * up).astype(BF16)
        part = _dot(mid, wd_ref[cols, :])
        y = part if y is None else y + part
    out_ref[...] = x + mod_ref[5:6, :] * _rms(y, gpost_ref[...])


def _ffn(x, mod, g_pre, wg, wu, wd, g_post, tm):
    bsz, s, d = x.shape
    ff = wg.shape[1]
    n_chunks = 2 if (ff // 2) % LANES == 0 else 1
    full = lambda a: pl.BlockSpec(a.shape, lambda b, i: (0,) * a.ndim)
    return pl.pallas_call(
        functools.partial(_ffn_kernel, n_chunks=n_chunks),
        out_shape=jax.ShapeDtypeStruct((bsz, s, d), F32),
        grid=(bsz, s // tm),
        in_specs=[pl.BlockSpec((None, tm, d), lambda b, i: (b, i, 0)),
                  pl.BlockSpec((None, N_MOD, d), lambda b, i: (b, 0, 0)),
                  full(g_pre), full(wg), full(wu), full(wd), full(g_post)],
        out_specs=pl.BlockSpec((None, tm, d), lambda b, i: (b, i, 0)),
        compiler_params=_cparams(("arbitrary", "arbitrary")),
        name="ffn",
    )(x, mod, g_pre, wg, wu, wd, g_post)


def _layer(x, mod, g_pre_mix, g_post_mix, w_in, ssm_a_re, ssm_a_im, ssm_log_dt, ssm_b_re, ssm_b_im,
           ssm_c_re, ssm_c_im, ssm_d, w_glu_a, w_glu_b, w_attn_out, w_out,
           g_pre_ffn, g_post_ffn, w_ff_gate, w_ff_up, w_ff_down):
    bsz, s, d = x.shape
    n_groups, n_state = ssm_a_re.shape
    ssm_w = n_groups * SSM_GROUP
    width = w_attn_out.shape[0]
    n_heads = width // HEAD_DIM
    assert 2 * bsz == SUBLANES, "SSM scan packs 2 sequence halves x batch into the 8 sublanes"
    tm = min(512, s // 2)
    steps_per_tile = min(64, s // 2)
    assert s % (2 * tm) == 0 and tm % MOBA_BLOCK == 0 and (s // 2) % steps_per_tile == 0

    mod3 = mod.reshape(bsz, N_MOD, d)
    row = lambda g: g.reshape(1, -1)

    ab_re, ab_im, bbt_re, bbt_im = _ssm_prep(ssm_a_re, ssm_a_im, ssm_log_dt, ssm_b_re, ssm_b_im)
    wb = jnp.concatenate([_block_diag_tiles(bbt_re), _block_diag_tiles(bbt_im)], axis=-1).astype(BF16)
    wc = jnp.concatenate([_block_diag_tiles(jnp.transpose(ssm_c_re, (0, 2, 1))),
                          _block_diag_tiles(-jnp.transpose(ssm_c_im, (0, 2, 1)))], axis=1).astype(BF16)
    a_cat = jnp.stack([ab_re.reshape(-1), ab_im.reshape(-1)], axis=0)
    a_cat = jnp.broadcast_to(a_cat[:, None, :], (2, SUBLANES, a_cat.shape[1]))
    d_row = ssm_d.reshape(1, ssm_w)

    u_tm, qt, ka, vt, sg = _proj(x, mod3, row(g_pre_mix), w_in, n_heads, ssm_w, tm)

    zero_state = jnp.zeros((2, SUBLANES, n_groups * n_state), F32)
    first = _ssm_pass(u_tm, wb, a_cat, wc, d_row, zero_state, steps_per_tile, emit=False)
    init = jnp.concatenate([jnp.zeros_like(first[:, :bsz]), first[:, :bsz]], axis=1)
    z_tm, _ = _ssm_pass(u_tm, wb, a_cat, wc, d_row, init, steps_per_tile, emit=True)

    o = _attention(qt, ka, vt)

    x1 = _merge(x, z_tm, o, sg, mod3, w_glu_a.astype(BF16), w_glu_b.astype(BF16),
                w_attn_out.astype(BF16), w_out.astype(BF16), row(g_post_mix), tm)
    return _ffn(x1, mod3, row(g_pre_ffn), w_ff_gate.astype(BF16), w_ff_up.astype(BF16),
                w_ff_down.astype(BF16), row(g_post_ffn), tm)


def kernel(x, c, w_ada, b_ada, g_pre_mix, g_post_mix, w_in, ssm_a_re, ssm_a_im, ssm_log_dt, ssm_b_re, ssm_b_im, ssm_c_re, ssm_c_im, ssm_d, w_glu_a, w_glu_b, w_attn_out, w_out, g_pre_ffn, g_post_ffn, w_ff_gate, w_ff_up, w_ff_down):
    depth = w_ada.shape[0]
    for l in range(depth):
        mod = _adaln(c, w_ada[l], b_ada[l])
        x = _layer(x, mod, g_pre_mix[l], g_post_mix[l], w_in[l], ssm_a_re[l], ssm_a_im[l], ssm_log_dt[l],
                   ssm_b_re[l], ssm_b_im[l], ssm_c_re[l], ssm_c_im[l], ssm_d[l], w_glu_a[l], w_glu_b[l],
                   w_attn_out[l], w_out[l], g_pre_ffn[l], g_post_ffn[l], w_ff_gate[l], w_ff_up[l], w_ff_down[l])
    return x
```

```python
import functools
import math

import jax
import jax.numpy as jnp
from jax import lax
from jax.experimental import pallas as pl
from jax.experimental.pallas import tpu as pltpu

F32 = jnp.float32
BF16 = jnp.bfloat16

SSM_GROUP = 16
SSM_STATE = 64
HEAD_DIM = 64
MOBA_BLOCK = 256
MOBA_TOPK = 3
N_MOD = 6
NORM_EPS = 1e-6
LANES = 128
SUBLANES = 8
GROUPS_PER_TILE = LANES // SSM_GROUP
NEG_BIG = -1e30
LOG2E = math.log2(math.e)
MASK_FEATS = 32
V_ROWS = HEAD_DIM + 16
KV_TILE = 2 * MOBA_BLOCK
TILES_PER_ITER = 4
VMEM_LIMIT = 56 * 1024 * 1024


def _cparams(sem):
    return pltpu.CompilerParams(dimension_semantics=sem, vmem_limit_bytes=VMEM_LIMIT)


def _dot(a, b):
    return jnp.dot(a, b, preferred_element_type=F32)


def _dot_nt(a, b):
    return lax.dot_general(a, b, (((1,), (1,)), ((), ())), preferred_element_type=F32)


def _adaln_kernel(c_ref, w_ref, b_ref, o_ref):
    c = c_ref[...]
    act = c * jax.nn.sigmoid(c)
    o_ref[...] = jnp.dot(act, w_ref[...], preferred_element_type=F32,
                         precision=lax.Precision.HIGHEST) + b_ref[...]


def _adaln(c, w_ada, b_ada):
    bsz, d = c.shape
    n = w_ada.shape[1]
    tn = n // N_MOD
    return pl.pallas_call(
        _adaln_kernel,
        out_shape=jax.ShapeDtypeStruct((bsz, n), F32),
        grid=(n // tn,),
        in_specs=[pl.BlockSpec((bsz, d), lambda j: (0, 0)),
                  pl.BlockSpec((d, tn), lambda j: (0, j)),
                  pl.BlockSpec((1, tn), lambda j: (0, j))],
        out_specs=pl.BlockSpec((bsz, tn), lambda j: (0, j)),
        compiler_params=_cparams(("arbitrary",)),
        name="adaln",
    )(c, w_ada, b_ada.reshape(1, n))


def _ssm_prep_kernel(are_ref, aim_ref, ldt_ref, bre_ref, bim_ref,
                     abre_ref, abim_ref, bbre_ref, bbim_ref):
    a_re = are_ref[...]
    a_im = aim_ref[...]
    dt = jnp.exp(ldt_ref[...])
    mag = jnp.exp(dt * a_re)
    ab_re = mag * jnp.cos(dt * a_im)
    ab_im = mag * jnp.sin(dt * a_im)
    den = a_re * a_re + a_im * a_im
    nr = ab_re - 1.0
    co_re = (nr * a_re + ab_im * a_im) / den
    co_im = (ab_im * a_re - nr * a_im) / den
    abre_ref[...] = ab_re
    abim_ref[...] = ab_im
    b_re = bre_ref[...]
    b_im = bim_ref[...]
    bbre_ref[...] = co_re[:, None, :] * b_re - co_im[:, None, :] * b_im
    bbim_ref[...] = co_re[:, None, :] * b_im + co_im[:, None, :] * b_re


def _ssm_prep(a_re, a_im, log_dt, b_re, b_im):
    g, p = a_re.shape
    c = b_re.shape[-1]
    bt_re = jnp.transpose(b_re, (0, 2, 1))
    bt_im = jnp.transpose(b_im, (0, 2, 1))
    return pl.pallas_call(
        _ssm_prep_kernel,
        out_shape=(jax.ShapeDtypeStruct((g, p), F32), jax.ShapeDtypeStruct((g, p), F32),
                   jax.ShapeDtypeStruct((g, c, p), F32), jax.ShapeDtypeStruct((g, c, p), F32)),
        name="ssm_prep",
    )(a_re, a_im, log_dt.reshape(g, 1), bt_re, bt_im)


def _block_diag_tiles(w):
    g, r, cc = w.shape
    nt = g // GROUPS_PER_TILE
    w = w.reshape(nt, GROUPS_PER_TILE, r, cc)
    eye = jnp.eye(GROUPS_PER_TILE, dtype=w.dtype)
    return jnp.einsum('sgrc,gh->sgrhc', w, eye).reshape(nt, GROUPS_PER_TILE * r, GROUPS_PER_TILE * cc)


def _alibi_slopes(n_heads):
    return [2.0 ** (-8.0 * (h + 1) / n_heads) for h in range(n_heads)]


def _proj_kernel(x_ref, mod_ref, gpre_ref, wu_ref, wqt_ref, wk_ref, wvt_ref, wg_ref, perm_ref,
                 u_ref, qt_ref, ka_ref, vt_ref, sg_ref, kmt_ref, *, n_heads, n_blk):
    i = pl.program_id(1)
    tm = x_ref.shape[0]
    width = n_heads * HEAD_DIM
    blk_per_tile = tm // MOBA_BLOCK

    @pl.when(i == 0)
    def _():
        kmt_ref[...] = jnp.zeros_like(kmt_ref)

    x = x_ref[...]
    ms = jnp.mean(x * x, axis=-1, keepdims=True)
    hn = x * lax.rsqrt(ms + NORM_EPS) * gpre_ref[...]
    hn = hn * (1.0 + mod_ref[1:2, :]) + mod_ref[0:1, :]
    hb = hn.astype(BF16)

    u_ref[...] = _dot(hb, wu_ref[...]).astype(BF16)
    q_t = _dot_nt(wqt_ref[...], hb)
    k = _dot(hb, wk_ref[...])
    v_t = _dot_nt(wvt_ref[...], hb)
    sg_ref[...] = jax.nn.sigmoid(_dot(hb, wg_ref[...])).astype(BF16)

    head_of_lane = lax.broadcasted_iota(jnp.int32, (n_heads, width), 1) // HEAD_DIM
    head_mask = (head_of_lane == lax.broadcasted_iota(jnp.int32, (n_heads, width), 0)).astype(F32)
    for lb in range(blk_per_tile):
        km = jnp.mean(k[lb * MOBA_BLOCK:(lb + 1) * MOBA_BLOCK, :], axis=0, keepdims=True)
        row0 = pl.multiple_of((i * blk_per_tile + lb) * n_heads, n_heads)
        kmt_ref[pl.ds(row0, n_heads), :] = km * head_mask

    gate = jnp.dot(kmt_ref[...], q_t, preferred_element_type=F32, precision=lax.Precision.HIGHEST)
    gate = gate.reshape(n_blk, n_heads, tm)
    n_iota = lax.broadcasted_iota(jnp.int32, (n_blk, n_heads, tm), 0)
    own = i * blk_per_tile + lax.broadcasted_iota(jnp.int32, (n_blk, n_heads, tm), 2) // MOBA_BLOCK
    neg_inf = jnp.float32(-jnp.inf)
    cur = jnp.where(n_iota < own, gate, neg_inf)
    sel = jnp.zeros((n_blk, n_heads, tm), F32)
    for _ in range(MOBA_TOPK):
        best = jnp.max(cur, axis=0, keepdims=True)
        idx = jnp.min(jnp.where(cur == best, n_iota, n_blk), axis=0, keepdims=True)
        hit = (n_iota == idx) & (best > neg_inf)
        sel = jnp.where(hit, 1.0, sel)
        cur = jnp.where(hit, neg_inf, cur)
    unsel = 1.0 - sel - (n_iota == own).astype(F32)
    unsel = unsel.reshape(n_blk * n_heads, tm).astype(BF16)
    mask_rows = _dot(perm_ref[...], unsel) * NEG_BIG

    sub8 = lax.broadcasted_iota(jnp.int32, (SUBLANES, tm), 0)
    ones3 = (sub8 < 3).astype(F32)
    ones1 = (sub8 < 1).astype(F32)
    zeros = lambda r: jnp.zeros((r, tm), F32)
    lane = lax.broadcasted_iota(jnp.int32, (tm, LANES), 1)
    row = lax.broadcasted_iota(jnp.int32, (tm, LANES), 0)
    own_row = i * blk_per_tile + row // MOBA_BLOCK
    pos = (row % MOBA_BLOCK).astype(F32)
    scale = HEAD_DIM ** -0.5 * LOG2E
    slopes = _alibi_slopes(n_heads)
    pad_feats = LANES - HEAD_DIM - MASK_FEATS - SUBLANES
    for h in range(n_heads):
        rows = slice(h * HEAD_DIM, (h + 1) * HEAD_DIM)
        tile = slice((h // 2) * LANES, (h // 2 + 1) * LANES)
        q_rows = q_t[rows] * scale
        m_rows = mask_rows[h * n_blk:(h + 1) * n_blk]
        if n_blk < MASK_FEATS:
            m_rows = jnp.concatenate([m_rows, zeros(MASK_FEATS - n_blk)], axis=0)
        bias = (slopes[h] * LOG2E) * pos
        bias_hi = bias.astype(BF16).astype(F32)
        bias_mid = (bias - bias_hi).astype(BF16).astype(F32)
        bias_lo = bias - bias_hi - bias_mid
        if h % 2 == 0:
            qt = jnp.concatenate([q_rows, m_rows, ones3, zeros(pad_feats)], axis=0)
            data_lanes = lane < HEAD_DIM
            feat0 = HEAD_DIM
        else:
            qt = jnp.concatenate([m_rows, ones3, zeros(pad_feats), q_rows], axis=0)
            data_lanes = lane >= HEAD_DIM
            feat0 = 0
        qt_ref[h] = qt.astype(BF16)
        feats = jnp.where(lane == feat0 + MASK_FEATS, bias_hi,
                          jnp.where(lane == feat0 + MASK_FEATS + 1, bias_mid,
                                    jnp.where(lane == feat0 + MASK_FEATS + 2, bias_lo,
                                              (lane == feat0 + own_row).astype(F32))))
        ka = jnp.where(data_lanes, k[:, tile], feats).astype(BF16)
        vt = jnp.concatenate([v_t[rows], ones1, zeros(V_ROWS - HEAD_DIM - SUBLANES)], axis=0).astype(BF16)
        for kt in range(tm // KV_TILE):
            ka_ref[h, kt] = ka[kt * KV_TILE:(kt + 1) * KV_TILE]
            vt_ref[h, kt] = vt[:, kt * KV_TILE:(kt + 1) * KV_TILE]


def _row_permutation(n_blk, n_heads):
    r = jnp.arange(n_heads * n_blk)
    src = (r % n_blk) * n_heads + r // n_blk
    return (src[:, None] == jnp.arange(n_blk * n_heads)[None, :]).astype(BF16)


def _proj(x, mod, g_pre, w_in, n_heads, ssm_w, tm):
    bsz, s, d = x.shape
    n_t = s // tm
    half_t = n_t // 2
    n_blk = s // MOBA_BLOCK
    width = n_heads * HEAD_DIM
    assert n_blk <= MASK_FEATS and tm % KV_TILE == 0
    c = [0, ssm_w, ssm_w + width, ssm_w + 2 * width, ssm_w + 3 * width, w_in.shape[1]]
    gate_w = c[5] - c[4]
    wu = w_in[:, c[0]:c[1]].astype(BF16)
    wqt = w_in[:, c[1]:c[2]].T.astype(BF16)
    wk = w_in[:, c[2]:c[3]].astype(BF16)
    wvt = w_in[:, c[3]:c[4]].T.astype(BF16)
    wg = w_in[:, c[4]:c[5]].astype(BF16)
    perm = _row_permutation(n_blk, n_heads)
    full = lambda a: pl.BlockSpec(a.shape, lambda b, i: (0,) * a.ndim)
    kern = functools.partial(_proj_kernel, n_heads=n_heads, n_blk=n_blk)
    return pl.pallas_call(
        kern,
        out_shape=(jax.ShapeDtypeStruct((s // 2, 2 * bsz * ssm_w), BF16),
                   jax.ShapeDtypeStruct((bsz, n_heads, LANES, s), BF16),
                   jax.ShapeDtypeStruct((bsz, n_heads, s // KV_TILE, KV_TILE, LANES), BF16),
                   jax.ShapeDtypeStruct((bsz, n_heads, s // KV_TILE, V_ROWS, KV_TILE), BF16),
                   jax.ShapeDtypeStruct((bsz, s, gate_w), BF16)),
        grid=(bsz, n_t),
        in_specs=[pl.BlockSpec((None, tm, d), lambda b, i: (b, i, 0)),
                  pl.BlockSpec((None, N_MOD, d), lambda b, i: (b, 0, 0)),
                  full(g_pre), full(wu), full(wqt), full(wk), full(wvt), full(wg), full(perm)],
        out_specs=(pl.BlockSpec((tm, ssm_w), lambda b, i: (i % half_t, (i // half_t) * bsz + b)),
                   pl.BlockSpec((None, n_heads, LANES, tm), lambda b, i: (b, 0, 0, i)),
                   pl.BlockSpec((None, n_heads, tm // KV_TILE, KV_TILE, LANES), lambda b, i: (b, 0, i, 0, 0)),
                   pl.BlockSpec((None, n_heads, tm // KV_TILE, V_ROWS, KV_TILE), lambda b, i: (b, 0, i, 0, 0)),
                   pl.BlockSpec((None, tm, gate_w), lambda b, i: (b, i, 0))),
        scratch_shapes=[pltpu.VMEM((n_blk * n_heads, width), F32)],
        compiler_params=_cparams(("arbitrary", "arbitrary")),
        name="proj",
    )(x, mod, g_pre, wu, wqt, wk, wvt, wg, perm)


def _gelu_tanh(x):
    c = math.sqrt(2.0 / math.pi)
    return 0.5 * x * (1.0 + jnp.tanh(c * (x + 0.044715 * (x * x * x))))


def _ssm_kernel(u_ref, un_ref, perm_ref, permt_ref, wb_ref, a_ref, wc_ref, d_ref, init_ref, *rest, emit):
    if emit:
        z_ref, fin_ref, bua_ref, bub_ref, usa_ref, usb_ref, st_ref, zs_ref = rest
    else:
        fin_ref, bua_ref, bub_ref, usa_ref, usb_ref, st_ref = rest
    i = pl.program_id(0)
    steps = un_ref.shape[0]
    ssm_w = u_ref.shape[1] // SUBLANES
    n_tiles = wb_ref.shape[0]
    st_w = wb_ref.shape[2] // 2

    def project(src_ref, t0, bu_ref, us_ref):
        u_nat = jnp.concatenate([src_ref[t0:t0 + steps, s * ssm_w:(s + 1) * ssm_w]
                                 for s in range(SUBLANES)], axis=0)
        us_ref[...] = _dot(perm_ref[...], u_nat).astype(BF16)
        for sg in range(n_tiles):
            r = _dot(us_ref[:, sg * LANES:(sg + 1) * LANES], wb_ref[sg])
            bu_ref[0, :, sg * st_w:(sg + 1) * st_w] = r[:, :st_w]
            bu_ref[1, :, sg * st_w:(sg + 1) * st_w] = r[:, st_w:]

    def scan(bu_ref, x_re, x_im):
        for t in range(steps):
            rows = slice(t * SUBLANES, (t + 1) * SUBLANES)
            a_re = a_ref[0]
            a_im = a_ref[1]
            n_re = a_re * x_re - a_im * x_im + bu_ref[0, rows, :]
            n_im = a_re * x_im + a_im * x_re + bu_ref[1, rows, :]
            bu_ref[0, rows, :] = n_re
            bu_ref[1, rows, :] = n_im
            x_re, x_im = n_re, n_im
        return x_re, x_im

    def output(bu_ref, us_ref, t0):
        for sg in range(n_tiles):
            xr = bu_ref[0, :, sg * st_w:(sg + 1) * st_w].astype(BF16)
            xi = bu_ref[1, :, sg * st_w:(sg + 1) * st_w].astype(BF16)
            y = _dot(xr, wc_ref[sg, :st_w, :]) + _dot(xi, wc_ref[sg, st_w:, :])
            lanes = slice(sg * LANES, (sg + 1) * LANES)
            y = y + d_ref[:, lanes] * us_ref[:, lanes].astype(F32)
            zs_ref[:, lanes] = _gelu_tanh(y).astype(BF16)
        z_nat = _dot(permt_ref[...], zs_ref[...]).astype(BF16)
        for s in range(SUBLANES):
            z_ref[t0:t0 + steps, s * ssm_w:(s + 1) * ssm_w] = z_nat[s * steps:(s + 1) * steps]

    @pl.when(i == 0)
    def _():
        st_ref[...] = init_ref[...]
        project(u_ref, 0, bua_ref, usa_ref)

    project(u_ref, steps, bub_ref, usb_ref)
    x_re, x_im = scan(bua_ref, st_ref[0], st_ref[1])
    if emit:
        output(bua_ref, usa_ref, 0)
    project(un_ref, 0, bua_ref, usa_ref)
    x_re, x_im = scan(bub_ref, x_re, x_im)
    if emit:
        output(bub_ref, usb_ref, steps)
    st_ref[0] = x_re
    st_ref[1] = x_im
    fin_ref[0] = x_re
    fin_ref[1] = x_im


def _ssm_pass(u_tm, wb, a_cat, wc, d_row, init, steps_per_tile, emit):
    n_steps, slot_w = u_tm.shape
    ssm_w = slot_w // SUBLANES
    rows = steps_per_tile * SUBLANES
    n_tiles = n_steps // steps_per_tile
    assert n_tiles % 2 == 0
    st_lanes = a_cat.shape[2]
    r = jnp.arange(rows)
    src = (r % SUBLANES) * steps_per_tile + r // SUBLANES
    perm = (src[:, None] == r[None, :]).astype(BF16)
    perm_t = perm.T
    fin_shape = jax.ShapeDtypeStruct((2, SUBLANES, st_lanes), F32)
    fin_spec = pl.BlockSpec((2, SUBLANES, st_lanes), lambda i: (0, 0, 0))
    scratch = [pltpu.VMEM((2, rows, st_lanes), F32), pltpu.VMEM((2, rows, st_lanes), F32),
               pltpu.VMEM((rows, ssm_w), BF16), pltpu.VMEM((rows, ssm_w), BF16),
               pltpu.VMEM((2, SUBLANES, st_lanes), F32)]
    if emit:
        out_shape = (jax.ShapeDtypeStruct((n_steps, slot_w), BF16), fin_shape)
        out_specs = (pl.BlockSpec((2 * steps_per_tile, slot_w), lambda i: (i, 0)), fin_spec)
        scratch.append(pltpu.VMEM((rows, ssm_w), BF16))
    else:
        out_shape = fin_shape
        out_specs = fin_spec
    return pl.pallas_call(
        functools.partial(_ssm_kernel, emit=emit),
        out_shape=out_shape,
        grid=(n_tiles // 2,),
        in_specs=[pl.BlockSpec((2 * steps_per_tile, slot_w), lambda i: (i, 0)),
                  pl.BlockSpec((steps_per_tile, slot_w), lambda i: (jnp.minimum(2 * i + 2, n_tiles - 1), 0)),
                  pl.BlockSpec(perm.shape, lambda i: (0, 0)),
                  pl.BlockSpec(perm.shape, lambda i: (0, 0)),
                  pl.BlockSpec(wb.shape, lambda i: (0, 0, 0)),
                  pl.BlockSpec(a_cat.shape, lambda i: (0, 0, 0)),
                  pl.BlockSpec(wc.shape, lambda i: (0, 0, 0)),
                  pl.BlockSpec(d_row.shape, lambda i: (0, 0)),
                  pl.BlockSpec(init.shape, lambda i: (0, 0, 0))],
        out_specs=out_specs,
        scratch_shapes=scratch,
        compiler_params=_cparams(("arbitrary",)),
        name="ssm_scan" if emit else "ssm_state",
    )(u_tm, u_tm, perm, perm_t, wb, a_cat, wc, d_row, init)


def _attn_kernel(slope_ref, qt_ref, ka_ref, vt_ref, o_ref, acc_ref, sa_ref, sb_ref):
    hp = pl.program_id(1)
    i = pl.program_id(2)
    tq = qt_ref.shape[2]
    key_row = lax.broadcasted_iota(jnp.int32, (KV_TILE, 1), 0)
    q_lane = lax.broadcasted_iota(jnp.int32, (1, tq), 1)
    causal = key_row <= q_lane
    acc_ref[...] = jnp.zeros_like(acc_ref)
    halves = KV_TILE // MOBA_BLOCK

    def scores(kt, s_ref):
        for hh in range(2):
            s_ref[hh] = _dot(ka_ref[hh, kt], qt_ref[hh])

    def col_max(s_t):
        rows = s_t.shape[0]
        while rows > SUBLANES:
            rows //= 2
            s_t = jnp.maximum(s_t[:rows], s_t[rows:])
        return jnp.max(s_t, axis=0, keepdims=True)

    def step(kt, s_ref, ms, last):
        new_ms = []
        for hh in range(2):
            slope = slope_ref[hp * 2 + hh]
            parts, shifts = [], []
            for d in range(halves):
                rows = slice(d * MOBA_BLOCK, (d + 1) * MOBA_BLOCK)
                s_t = s_ref[hh, rows, :]
                if last:
                    s_t = jnp.where(causal[rows], s_t, NEG_BIG)
                parts.append(s_t)
                shifts.append(slope * (((kt - i) * halves + d) * MOBA_BLOCK).astype(F32))
            m_new = ms[hh]
            for s_t, shift in zip(parts, shifts):
                m_new = jnp.maximum(m_new, col_max(s_t) + shift)
            p = jnp.concatenate([jnp.exp2(s_t - (m_new - shift)) for s_t, shift in zip(parts, shifts)],
                                axis=0).astype(BF16)
            acc_ref[hh] = acc_ref[hh] * jnp.exp2(ms[hh] - m_new) + _dot(vt_ref[hh, kt], p)
            new_ms.append(m_new)
        return tuple(new_ms)

    m0 = jnp.full((1, tq), 0.1 * NEG_BIG, F32)
    last_kt = i

    def run_tiles(first, count, ms, ends_sequence):
        bufs = (sa_ref, sb_ref)
        for j in range(count):
            if j + 1 < count or not ends_sequence:
                scores(first + j + 1, bufs[(j + 1) % 2])
            ms = step(first + j, bufs[j % 2], ms, ends_sequence and j + 1 == count)
        return ms

    scores(0, sa_ref)
    ms = lax.fori_loop(0, last_kt // TILES_PER_ITER,
                       lambda q, ms: run_tiles(q * TILES_PER_ITER, TILES_PER_ITER, ms, False), (m0, m0))
    rem = last_kt % TILES_PER_ITER
    for k in range(TILES_PER_ITER):
        @pl.when(rem == k)
        def _(k=k):
            run_tiles(last_kt - k, k + 1, ms, True)


    outs = []
    for hh in range(2):
        acc = acc_ref[hh]
        outs.append(acc[:HEAD_DIM] / acc[HEAD_DIM:HEAD_DIM + 1])
    o_ref[...] = jnp.transpose(jnp.concatenate(outs, axis=0)).astype(o_ref.dtype)


def _attention(qt, ka, vt):
    bsz, n_heads, _, s = qt.shape
    n_kt = ka.shape[2]
    tq = KV_TILE
    slopes = jnp.asarray([sl * LOG2E for sl in _alibi_slopes(n_heads)], F32)
    return pl.pallas_call(
        _attn_kernel,
        out_shape=jax.ShapeDtypeStruct((bsz, s, n_heads * HEAD_DIM), BF16),
        grid=(bsz, n_heads // 2, n_kt),
        in_specs=[pl.BlockSpec(memory_space=pltpu.SMEM),
                  pl.BlockSpec((None, 2, LANES, tq), lambda b, hp, i: (b, hp, 0, i)),
                  pl.BlockSpec((None, 2, n_kt, KV_TILE, LANES), lambda b, hp, i: (b, hp, 0, 0, 0)),
                  pl.BlockSpec((None, 2, n_kt, V_ROWS, KV_TILE), lambda b, hp, i: (b, hp, 0, 0, 0))],
        out_specs=pl.BlockSpec((None, tq, LANES), lambda b, hp, i: (b, i, hp)),
        scratch_shapes=[pltpu.VMEM((2, V_ROWS, tq), F32),
                        pltpu.VMEM((2, KV_TILE, tq), F32),
                        pltpu.VMEM((2, KV_TILE, tq), F32)],
        compiler_params=_cparams(("arbitrary", "arbitrary", "arbitrary")),
        name="attn",
    )(slopes, qt, ka, vt)


def _rms(y, g):
    return y * lax.rsqrt(jnp.mean(y * y, axis=-1, keepdims=True) + NORM_EPS) * g


def _merge_kernel(x_ref, z_ref, o_ref, sg_ref, mod_ref, wa_ref, wb_ref, wo_ref, wout_ref, gpost_ref, out_ref):
    d = x_ref.shape[1]
    z = z_ref[...]
    y_a = _dot(z, wa_ref[...]) * jax.nn.sigmoid(_dot(z, wb_ref[...]))
    y_b = _dot(o_ref[...], wo_ref[...])
    merged = sg_ref[:, :d].astype(F32) * y_a + sg_ref[:, d:].astype(F32) * y_b
    y = _dot(merged.astype(BF16), wout_ref[...])
    out_ref[...] = x_ref[...] + mod_ref[2:3, :] * _rms(y, gpost_ref[...])


def _merge(x, z_tm, o, sg, mod, wa, wb, wo, wout, g_post, tm):
    bsz, s, d = x.shape
    n_t = s // tm
    half_t = n_t // 2
    ssm_w = wa.shape[0]
    full = lambda a: pl.BlockSpec(a.shape, lambda b, i: (0,) * a.ndim)
    return pl.pallas_call(
        _merge_kernel,
        out_shape=jax.ShapeDtypeStruct((bsz, s, d), F32),
        grid=(bsz, n_t),
        in_specs=[pl.BlockSpec((None, tm, d), lambda b, i: (b, i, 0)),
                  pl.BlockSpec((tm, ssm_w), lambda b, i: (i % half_t, (i // half_t) * bsz + b)),
                  pl.BlockSpec((None, tm, o.shape[2]), lambda b, i: (b, i, 0)),
                  pl.BlockSpec((None, tm, sg.shape[2]), lambda b, i: (b, i, 0)),
                  pl.BlockSpec((None, N_MOD, d), lambda b, i: (b, 0, 0)),
                  full(wa), full(wb), full(wo), full(wout), full(g_post)],
        out_specs=pl.BlockSpec((None, tm, d), lambda b, i: (b, i, 0)),
        compiler_params=_cparams(("arbitrary", "arbitrary")),
        name="merge",
    )(x, z_tm, o, sg, mod, wa, wb, wo, wout, g_post)


def _ffn_kernel(x_ref, mod_ref, gpre_ref, wg_ref, wu_ref, wd_ref, gpost_ref, out_ref, *, n_chunks):
    x = x_ref[...]
    hn = _rms(x, gpre_ref[...]) * (1.0 + mod_ref[4:5, :]) + mod_ref[3:4, :]
    hb = hn.astype(BF16)
    ff = wg_ref.shape[1]
    cw = ff // n_chunks
    y = None
    for c in range(n_chunks):
        cols = slice(c * cw, (c + 1) * cw)
        gate = _dot(hb, wg_ref[:, cols])
        up = _dot(hb, wu_ref[:, cols])
        mid = (gate * jax.nn.sigmoid(gate) * up).astype(BF16)
        part = _dot(mid, wd_ref[cols, :])
        y = part if y is None else y + part
    out_ref[...] = x + mod_ref[5:6, :] * _rms(y, gpost_ref[...])


def _ffn(x, mod, g_pre, wg, wu, wd, g_post, tm):
    bsz, s, d = x.shape
    ff = wg.shape[1]
    n_chunks = 2 if (ff // 2) % LANES == 0 else 1
    full = lambda a: pl.BlockSpec(a.shape, lambda b, i: (0,) * a.ndim)
    return pl.pallas_call(
        functools.partial(_ffn_kernel, n_chunks=n_chunks),
        out_shape=jax.ShapeDtypeStruct((bsz, s, d), F32),
        grid=(bsz, s // tm),
        in_specs=[pl.BlockSpec((None, tm, d), lambda b, i: (b, i, 0)),
                  pl.BlockSpec((None, N_MOD, d), lambda b, i: (b, 0, 0)),
                  full(g_pre), full(wg), full(wu), full(wd), full(g_post)],
        out_specs=pl.BlockSpec((None, tm, d), lambda b, i: (b, i, 0)),
        compiler_params=_cparams(("arbitrary", "arbitrary")),
        name="ffn",
    )(x, mod, g_pre, wg, wu, wd, g_post)


def _layer(x, mod, g_pre_mix, g_post_mix, w_in, ssm_a_re, ssm_a_im, ssm_log_dt, ssm_b_re, ssm_b_im,
           ssm_c_re, ssm_c_im, ssm_d, w_glu_a, w_glu_b, w_attn_out, w_out,
           g_pre_ffn, g_post_ffn, w_ff_gate, w_ff_up, w_ff_down):
    bsz, s, d = x.shape
    n_groups, n_state = ssm_a_re.shape
    ssm_w = n_groups * SSM_GROUP
    width = w_attn_out.shape[0]
    n_heads = width // HEAD_DIM
    assert 2 * bsz == SUBLANES, "SSM scan packs 2 sequence halves x batch into the 8 sublanes"
    tm = min(512, s // 2)
    steps_per_tile = min(64, s // 2)
    assert s % (2 * tm) == 0 and tm % MOBA_BLOCK == 0 and (s // 2) % steps_per_tile == 0

    mod3 = mod.reshape(bsz, N_MOD, d)
    row = lambda g: g.reshape(1, -1)

    ab_re, ab_im, bbt_re, bbt_im = _ssm_prep(ssm_a_re, ssm_a_im, ssm_log_dt, ssm_b_re, ssm_b_im)
    wb = jnp.concatenate([_block_diag_tiles(bbt_re), _block_diag_tiles(bbt_im)], axis=-1).astype(BF16)
    wc = jnp.concatenate([_block_diag_tiles(jnp.transpose(ssm_c_re, (0, 2, 1))),
                          _block_diag_tiles(-jnp.transpose(ssm_c_im, (0, 2, 1)))], axis=1).astype(BF16)
    a_cat = jnp.stack([ab_re.reshape(-1), ab_im.reshape(-1)], axis=0)
    a_cat = jnp.broadcast_to(a_cat[:, None, :], (2, SUBLANES, a_cat.shape[1]))
    d_row = ssm_d.reshape(1, ssm_w)

    u_tm, qt, ka, vt, sg = _proj(x, mod3, row(g_pre_mix), w_in, n_heads, ssm_w, tm)

    zero_state = jnp.zeros((2, SUBLANES, n_groups * n_state), F32)
    first = _ssm_pass(u_tm, wb, a_cat, wc, d_row, zero_state, steps_per_tile, emit=False)
    init = jnp.concatenate([jnp.zeros_like(first[:, :bsz]), first[:, :bsz]], axis=1)
    z_tm, _ = _ssm_pass(u_tm, wb, a_cat, wc, d_row, init, steps_per_tile, emit=True)

    o = _attention(qt, ka, vt)

    x1 = _merge(x, z_tm, o, sg, mod3, w_glu_a.astype(BF16), w_glu_b.astype(BF16),
                w_attn_out.astype(BF16), w_out.astype(BF16), row(g_post_mix), tm)
    return _ffn(x1, mod3, row(g_pre_ffn), w_ff_gate.astype(BF16), w_ff_up.astype(BF16),
                w_ff_down.astype(BF16), row(g_post_ffn), tm)


def kernel(x, c, w_ada, b_ada, g_pre_mix, g_post_mix, w_in, ssm_a_re, ssm_a_im, ssm_log_dt, ssm_b_re, ssm_b_im, ssm_c_re, ssm_c_im, ssm_d, w_glu_a, w_glu_b, w_attn_out, w_out, g_pre_ffn, g_post_ffn, w_ff_gate, w_ff_up, w_ff_down):
    depth = w_ada.shape[0]
    for l in range(depth):
        mod = _adaln(c, w_ada[l], b_ada[l])
        x = _layer(x, mod, g_pre_mix[l], g_post_mix[l], w_in[l], ssm_a_re[l], ssm_a_im[l], ssm_log_dt[l],
                   ssm_b_re[l], ssm_b_im[l], ssm_c_re[l], ssm_c_im[l], ssm_d[l], w_glu_a[l], w_glu_b[l],
                   w_attn_out[l], w_out[l], g_pre_ffn[l], g_post_ffn[l], w_ff_gate[l], w_ff_up[l], w_ff_down[l])
    return x
```

```python
import functools
import math

import jax
import jax.numpy as jnp
from jax import lax
from jax.experimental import pallas as pl
from jax.experimental.pallas import tpu as pltpu

F32 = jnp.float32
BF16 = jnp.bfloat16

SSM_GROUP = 16
SSM_STATE = 64
HEAD_DIM = 64
MOBA_BLOCK = 256
MOBA_TOPK = 3
N_MOD = 6
NORM_EPS = 1e-6
LANES = 128
SUBLANES = 8
GROUPS_PER_TILE = LANES // SSM_GROUP
NEG_BIG = -1e30
LOG2E = math.log2(math.e)
MASK_FEATS = 32
V_ROWS = HEAD_DIM + 16
KV_TILE = 2 * MOBA_BLOCK
TILES_PER_ITER = 8
VMEM_LIMIT = 56 * 1024 * 1024


def _cparams(sem):
    return pltpu.CompilerParams(dimension_semantics=sem, vmem_limit_bytes=VMEM_LIMIT)


def _dot(a, b):
    return jnp.dot(a, b, preferred_element_type=F32)


def _dot_nt(a, b):
    return lax.dot_general(a, b, (((1,), (1,)), ((), ())), preferred_element_type=F32)


def _adaln_kernel(c_ref, w_ref, b_ref, o_ref):
    c = c_ref[...]
    act = c * jax.nn.sigmoid(c)
    o_ref[...] = jnp.dot(act, w_ref[...], preferred_element_type=F32,
                         precision=lax.Precision.HIGHEST) + b_ref[...]


def _adaln(c, w_ada, b_ada):
    bsz, d = c.shape
    n = w_ada.shape[1]
    tn = n // N_MOD
    return pl.pallas_call(
        _adaln_kernel,
        out_shape=jax.ShapeDtypeStruct((bsz, n), F32),
        grid=(n // tn,),
        in_specs=[pl.BlockSpec((bsz, d), lambda j: (0, 0)),
                  pl.BlockSpec((d, tn), lambda j: (0, j)),
                  pl.BlockSpec((1, tn), lambda j: (0, j))],
        out_specs=pl.BlockSpec((bsz, tn), lambda j: (0, j)),
        compiler_params=_cparams(("arbitrary",)),
        name="adaln",
    )(c, w_ada, b_ada.reshape(1, n))


def _ssm_prep_kernel(are_ref, aim_ref, ldt_ref, bre_ref, bim_ref,
                     abre_ref, abim_ref, bbre_ref, bbim_ref):
    a_re = are_ref[...]
    a_im = aim_ref[...]
    dt = jnp.exp(ldt_ref[...])
    mag = jnp.exp(dt * a_re)
    ab_re = mag * jnp.cos(dt * a_im)
    ab_im = mag * jnp.sin(dt * a_im)
    den = a_re * a_re + a_im * a_im
    nr = ab_re - 1.0
    co_re = (nr * a_re + ab_im * a_im) / den
    co_im = (ab_im * a_re - nr * a_im) / den
    abre_ref[...] = ab_re
    abim_ref[...] = ab_im
    b_re = bre_ref[...]
    b_im = bim_ref[...]
    bbre_ref[...] = co_re[:, None, :] * b_re - co_im[:, None, :] * b_im
    bbim_ref[...] = co_re[:, None, :] * b_im + co_im[:, None, :] * b_re


def _ssm_prep(a_re, a_im, log_dt, b_re, b_im):
    g, p = a_re.shape
    c = b_re.shape[-1]
    bt_re = jnp.transpose(b_re, (0, 2, 1))
    bt_im = jnp.transpose(b_im, (0, 2, 1))
    return pl.pallas_call(
        _ssm_prep_kernel,
        out_shape=(jax.ShapeDtypeStruct((g, p), F32), jax.ShapeDtypeStruct((g, p), F32),
                   jax.ShapeDtypeStruct((g, c, p), F32), jax.ShapeDtypeStruct((g, c, p), F32)),
        name="ssm_prep",
    )(a_re, a_im, log_dt.reshape(g, 1), bt_re, bt_im)


def _block_diag_tiles(w):
    g, r, cc = w.shape
    nt = g // GROUPS_PER_TILE
    w = w.reshape(nt, GROUPS_PER_TILE, r, cc)
    eye = jnp.eye(GROUPS_PER_TILE, dtype=w.dtype)
    return jnp.einsum('sgrc,gh->sgrhc', w, eye).reshape(nt, GROUPS_PER_TILE * r, GROUPS_PER_TILE * cc)


def _alibi_slopes(n_heads):
    return [2.0 ** (-8.0 * (h + 1) / n_heads) for h in range(n_heads)]


def _proj_kernel(x_ref, mod_ref, gpre_ref, wu_ref, wqt_ref, wk_ref, wvt_ref, wg_ref, perm_ref, kbias_ref,
                 u_ref, qt_ref, ka_ref, vt_ref, sg_ref, kmt_ref, *, n_heads, n_blk):
    i = pl.program_id(1)
    tm = x_ref.shape[0]
    width = n_heads * HEAD_DIM
    blk_per_tile = tm // MOBA_BLOCK

    @pl.when(i == 0)
    def _():
        kmt_ref[...] = jnp.zeros_like(kmt_ref)

    x = x_ref[...]
    ms = jnp.mean(x * x, axis=-1, keepdims=True)
    gain = gpre_ref[...] * (1.0 + mod_ref[1:2, :])
    hn = x * lax.rsqrt(ms + NORM_EPS) * gain + mod_ref[0:1, :]
    hb = hn.astype(BF16)

    q_t = _dot_nt(wqt_ref[...], hb)
    k = _dot(hb, wk_ref[...])

    head_of_lane = lax.broadcasted_iota(jnp.int32, (n_heads, width), 1) // HEAD_DIM
    head_mask = (head_of_lane == lax.broadcasted_iota(jnp.int32, (n_heads, width), 0)).astype(F32)
    for lb in range(blk_per_tile):
        km = jnp.mean(k[lb * MOBA_BLOCK:(lb + 1) * MOBA_BLOCK, :], axis=0, keepdims=True)
        row0 = pl.multiple_of((i * blk_per_tile + lb) * n_heads, n_heads)
        kmt_ref[pl.ds(row0, n_heads), :] = km * head_mask

    gate = jnp.dot(kmt_ref[...], q_t, preferred_element_type=F32, precision=lax.Precision.HIGHEST)
    u_ref[...] = _dot(hb, wu_ref[...]).astype(BF16)
    v_t = _dot_nt(wvt_ref[...], hb)
    gate = gate.reshape(n_blk, n_heads, tm)
    n_iota = lax.broadcasted_iota(jnp.int32, (n_blk, n_heads, tm), 0)
    own = i * blk_per_tile + lax.broadcasted_iota(jnp.int32, (n_blk, n_heads, tm), 2) // MOBA_BLOCK
    neg_inf = jnp.float32(-jnp.inf)
    cur = jnp.where(n_iota < own, gate, neg_inf)
    sel = jnp.zeros((n_blk, n_heads, tm), F32)
    for _ in range(MOBA_TOPK):
        best = jnp.max(cur, axis=0, keepdims=True)
        idx = jnp.min(jnp.where(cur == best, n_iota, n_blk), axis=0, keepdims=True)
        hit = (n_iota == idx) & (best > neg_inf)
        sel = jnp.where(hit, 1.0, sel)
        cur = jnp.where(hit, neg_inf, cur)
    unsel = 1.0 - sel - (n_iota == own).astype(F32)
    unsel = unsel.reshape(n_blk * n_heads, tm).astype(BF16)
    mask_rows = _dot(perm_ref[...], unsel) * NEG_BIG
    sg_ref[...] = jax.nn.sigmoid(_dot(hb, wg_ref[...])).astype(BF16)

    sub8 = lax.broadcasted_iota(jnp.int32, (SUBLANES, tm), 0)
    ones3 = (sub8 < 3).astype(F32)
    ones1 = (sub8 < 1).astype(F32)
    zeros = lambda r: jnp.zeros((r, tm), F32)
    lane = lax.broadcasted_iota(jnp.int32, (tm, LANES), 1)
    row = lax.broadcasted_iota(jnp.int32, (tm, LANES), 0)
    own_row = i * blk_per_tile + row // MOBA_BLOCK
    scale = HEAD_DIM ** -0.5 * LOG2E
    slopes = _alibi_slopes(n_heads)
    pad_feats = LANES - HEAD_DIM - MASK_FEATS - SUBLANES
    for h in range(n_heads):
        rows = slice(h * HEAD_DIM, (h + 1) * HEAD_DIM)
        tile = slice((h // 2) * LANES, (h // 2 + 1) * LANES)
        q_rows = q_t[rows] * scale
        m_rows = mask_rows[h * n_blk:(h + 1) * n_blk]
        if n_blk < MASK_FEATS:
            m_rows = jnp.concatenate([m_rows, zeros(MASK_FEATS - n_blk)], axis=0)
        if h % 2 == 0:
            qt = jnp.concatenate([q_rows, m_rows, ones3, zeros(pad_feats)], axis=0)
            data_lanes = lane < HEAD_DIM
            feat0 = HEAD_DIM
        else:
            qt = jnp.concatenate([m_rows, ones3, zeros(pad_feats), q_rows], axis=0)
            data_lanes = lane >= HEAD_DIM
            feat0 = 0
        qt_ref[h] = qt.astype(BF16)
        feats = jnp.where(lane == feat0 + own_row, 1.0, kbias_ref[h])
        ka = jnp.where(data_lanes, k[:, tile], feats).astype(BF16)
        vt = jnp.concatenate([v_t[rows], ones1, zeros(V_ROWS - HEAD_DIM - SUBLANES)], axis=0).astype(BF16)
        for kt in range(tm // KV_TILE):
            ka_ref[h, kt] = ka[kt * KV_TILE:(kt + 1) * KV_TILE]
            vt_ref[h, kt] = vt[:, kt * KV_TILE:(kt + 1) * KV_TILE]


def _alibi_key_features(n_heads, tm):
    pos = (jnp.arange(tm) % MOBA_BLOCK).astype(F32)
    lane = jnp.arange(LANES)
    out = []
    for h, slope in enumerate(_alibi_slopes(n_heads)):
        feat0 = (HEAD_DIM if h % 2 == 0 else 0) + MASK_FEATS
        bias = (slope * LOG2E) * pos
        hi = lax.reduce_precision(bias, exponent_bits=8, mantissa_bits=7)
        mid = lax.reduce_precision(bias - hi, exponent_bits=8, mantissa_bits=7)
        lo = bias - hi - mid
        feats = jnp.zeros((tm, LANES), F32)
        for j, term in enumerate((hi, mid, lo)):
            feats = jnp.where(lane[None, :] == feat0 + j, term[:, None], feats)
        out.append(feats)
    return jnp.stack(out, axis=0)


def _row_permutation(n_blk, n_heads):
    r = jnp.arange(n_heads * n_blk)
    src = (r % n_blk) * n_heads + r // n_blk
    return (src[:, None] == jnp.arange(n_blk * n_heads)[None, :]).astype(BF16)


def _proj(x, mod, g_pre, w_in, n_heads, ssm_w, tm):
    bsz, s, d = x.shape
    n_t = s // tm
    half_t = n_t // 2
    n_blk = s // MOBA_BLOCK
    width = n_heads * HEAD_DIM
    assert n_blk <= MASK_FEATS and tm % KV_TILE == 0
    c = [0, ssm_w, ssm_w + width, ssm_w + 2 * width, ssm_w + 3 * width, w_in.shape[1]]
    gate_w = c[5] - c[4]
    wu = w_in[:, c[0]:c[1]].astype(BF16)
    wqt = w_in[:, c[1]:c[2]].T.astype(BF16)
    wk = w_in[:, c[2]:c[3]].astype(BF16)
    wvt = w_in[:, c[3]:c[4]].T.astype(BF16)
    wg = w_in[:, c[4]:c[5]].astype(BF16)
    perm = _row_permutation(n_blk, n_heads)
    kbias = _alibi_key_features(n_heads, tm)
    full = lambda a: pl.BlockSpec(a.shape, lambda b, i: (0,) * a.ndim)
    kern = functools.partial(_proj_kernel, n_heads=n_heads, n_blk=n_blk)
    return pl.pallas_call(
        kern,
        out_shape=(jax.ShapeDtypeStruct((s // 2, 2 * bsz * ssm_w), BF16),
                   jax.ShapeDtypeStruct((bsz, n_heads, LANES, s), BF16),
                   jax.ShapeDtypeStruct((bsz, n_heads, s // KV_TILE, KV_TILE, LANES), BF16),
                   jax.ShapeDtypeStruct((bsz, n_heads, s // KV_TILE, V_ROWS, KV_TILE), BF16),
                   jax.ShapeDtypeStruct((bsz, s, gate_w), BF16)),
        grid=(bsz, n_t),
        in_specs=[pl.BlockSpec((None, tm, d), lambda b, i: (b, i, 0)),
                  pl.BlockSpec((None, N_MOD, d), lambda b, i: (b, 0, 0)),
                  full(g_pre), full(wu), full(wqt), full(wk), full(wvt), full(wg), full(perm), full(kbias)],
        out_specs=(pl.BlockSpec((tm, ssm_w), lambda b, i: (i % half_t, (i // half_t) * bsz + b)),
                   pl.BlockSpec((None, n_heads, LANES, tm), lambda b, i: (b, 0, 0, i)),
                   pl.BlockSpec((None, n_heads, tm // KV_TILE, KV_TILE, LANES), lambda b, i: (b, 0, i, 0, 0)),
                   pl.BlockSpec((None, n_heads, tm // KV_TILE, V_ROWS, KV_TILE), lambda b, i: (b, 0, i, 0, 0)),
                   pl.BlockSpec((None, tm, gate_w), lambda b, i: (b, i, 0))),
        scratch_shapes=[pltpu.VMEM((n_blk * n_heads, width), F32)],
        compiler_params=_cparams(("arbitrary", "arbitrary")),
        name="proj",
    )(x, mod, g_pre, wu, wqt, wk, wvt, wg, perm, kbias)


def _gelu_tanh(x):
    c = math.sqrt(2.0 / math.pi)
    return 0.5 * x * (1.0 + jnp.tanh(c * (x + 0.044715 * (x * x * x))))


def _ssm_kernel(u_ref, un_ref, perm_ref, permt_ref, wb_ref, a_ref, wc_ref, d_ref, init_ref, *rest, emit):
    if emit:
        z_ref, fin_ref, bua_ref, bub_ref, usa_ref, usb_ref, st_ref, zs_ref = rest
    else:
        fin_ref, bua_ref, bub_ref, usa_ref, usb_ref, st_ref = rest
    i = pl.program_id(0)
    steps = un_ref.shape[0]
    ssm_w = u_ref.shape[1] // SUBLANES
    n_tiles = wb_ref.shape[0]
    st_w = wb_ref.shape[2] // 2

    def project(src_ref, t0, bu_ref, us_ref):
        u_nat = jnp.concatenate([src_ref[t0:t0 + steps, s * ssm_w:(s + 1) * ssm_w]
                                 for s in range(SUBLANES)], axis=0)
        us_ref[...] = _dot(perm_ref[...], u_nat).astype(BF16)
        for sg in range(n_tiles):
            r = _dot(us_ref[:, sg * LANES:(sg + 1) * LANES], wb_ref[sg])
            bu_ref[0, :, sg * st_w:(sg + 1) * st_w] = r[:, :st_w]
            bu_ref[1, :, sg * st_w:(sg + 1) * st_w] = r[:, st_w:]

    def scan(bu_ref, x_re, x_im):
        for t in range(steps):
            rows = slice(t * SUBLANES, (t + 1) * SUBLANES)
            a_re = a_ref[0]
            a_im = a_ref[1]
            n_re = a_re * x_re - a_im * x_im + bu_ref[0, rows, :]
            n_im = a_re * x_im + a_im * x_re + bu_ref[1, rows, :]
            bu_ref[0, rows, :] = n_re
            bu_ref[1, rows, :] = n_im
            x_re, x_im = n_re, n_im
        return x_re, x_im

    def output(bu_ref, us_ref, t0):
        for sg in range(n_tiles):
            xr = bu_ref[0, :, sg * st_w:(sg + 1) * st_w].astype(BF16)
            xi = bu_ref[1, :, sg * st_w:(sg + 1) * st_w].astype(BF16)
            y = _dot(xr, wc_ref[sg, :st_w, :]) + _dot(xi, wc_ref[sg, st_w:, :])
            lanes = slice(sg * LANES, (sg + 1) * LANES)
            y = y + d_ref[:, lanes] * us_ref[:, lanes].astype(F32)
            zs_ref[:, lanes] = _gelu_tanh(y).astype(BF16)
        z_nat = _dot(permt_ref[...], zs_ref[...]).astype(BF16)
        for s in range(SUBLANES):
            z_ref[t0:t0 + steps, s * ssm_w:(s + 1) * ssm_w] = z_nat[s * steps:(s + 1) * steps]

    @pl.when(i == 0)
    def _():
        st_ref[...] = init_ref[...]
        project(u_ref, 0, bua_ref, usa_ref)

    project(u_ref, steps, bub_ref, usb_ref)
    x_re, x_im = scan(bua_ref, st_ref[0], st_ref[1])
    if emit:
        output(bua_ref, usa_ref, 0)
    project(un_ref, 0, bua_ref, usa_ref)
    x_re, x_im = scan(bub_ref, x_re, x_im)
    if emit:
        output(bub_ref, usb_ref, steps)
    st_ref[0] = x_re
    st_ref[1] = x_im
    fin_ref[0] = x_re
    fin_ref[1] = x_im


def _ssm_pass(u_tm, wb, a_cat, wc, d_row, init, steps_per_tile, emit):
    n_steps, slot_w = u_tm.shape
    ssm_w = slot_w // SUBLANES
    rows = steps_per_tile * SUBLANES
    n_tiles = n_steps // steps_per_tile
    assert n_tiles % 2 == 0
    st_lanes = a_cat.shape[2]
    r = jnp.arange(rows)
    src = (r % SUBLANES) * steps_per_tile + r // SUBLANES
    perm = (src[:, None] == r[None, :]).astype(BF16)
    perm_t = perm.T
    fin_shape = jax.ShapeDtypeStruct((2, SUBLANES, st_lanes), F32)
    fin_spec = pl.BlockSpec((2, SUBLANES, st_lanes), lambda i: (0, 0, 0))
    scratch = [pltpu.VMEM((2, rows, st_lanes), F32), pltpu.VMEM((2, rows, st_lanes), F32),
               pltpu.VMEM((rows, ssm_w), BF16), pltpu.VMEM((rows, ssm_w), BF16),
               pltpu.VMEM((2, SUBLANES, st_lanes), F32)]
    if emit:
        out_shape = (jax.ShapeDtypeStruct((n_steps, slot_w), BF16), fin_shape)
        out_specs = (pl.BlockSpec((2 * steps_per_tile, slot_w), lambda i: (i, 0)), fin_spec)
        scratch.append(pltpu.VMEM((rows, ssm_w), BF16))
    else:
        out_shape = fin_shape
        out_specs = fin_spec
    return pl.pallas_call(
        functools.partial(_ssm_kernel, emit=emit),
        out_shape=out_shape,
        grid=(n_tiles // 2,),
        in_specs=[pl.BlockSpec((2 * steps_per_tile, slot_w), lambda i: (i, 0)),
                  pl.BlockSpec((steps_per_tile, slot_w), lambda i: (jnp.minimum(2 * i + 2, n_tiles - 1), 0)),
                  pl.BlockSpec(perm.shape, lambda i: (0, 0)),
                  pl.BlockSpec(perm.shape, lambda i: (0, 0)),
                  pl.BlockSpec(wb.shape, lambda i: (0, 0, 0)),
                  pl.BlockSpec(a_cat.shape, lambda i: (0, 0, 0)),
                  pl.BlockSpec(wc.shape, lambda i: (0, 0, 0)),
                  pl.BlockSpec(d_row.shape, lambda i: (0, 0)),
                  pl.BlockSpec(init.shape, lambda i: (0, 0, 0))],
        out_specs=out_specs,
        scratch_shapes=scratch,
        compiler_params=_cparams(("arbitrary",)),
        name="ssm_scan" if emit else "ssm_state",
    )(u_tm, u_tm, perm, perm_t, wb, a_cat, wc, d_row, init)


def _attn_kernel(slope_ref, qt_ref, ka_ref, vt_ref, o_ref, acc_ref, sa_ref, sb_ref):
    hp = pl.program_id(1)
    i = pl.program_id(2)
    tq = qt_ref.shape[2]
    key_row = lax.broadcasted_iota(jnp.int32, (KV_TILE, 1), 0)
    q_lane = lax.broadcasted_iota(jnp.int32, (1, tq), 1)
    causal = key_row <= q_lane
    acc_ref[...] = jnp.zeros_like(acc_ref)
    halves = KV_TILE // MOBA_BLOCK

    def scores(kt, s_ref):
        for hh in range(2):
            s_ref[hh] = _dot(ka_ref[hh, kt], qt_ref[hh])

    def col_max(s_t):
        rows = s_t.shape[0]
        while rows > SUBLANES:
            rows //= 2
            s_t = jnp.maximum(s_t[:rows], s_t[rows:])
        return jnp.max(s_t, axis=0, keepdims=True)

    def step(kt, s_ref, ms, last):
        new_ms = []
        for hh in range(2):
            slope = slope_ref[hp * 2 + hh]
            parts, shifts = [], []
            for d in range(halves):
                rows = slice(d * MOBA_BLOCK, (d + 1) * MOBA_BLOCK)
                s_t = s_ref[hh, rows, :]
                if last:
                    s_t = jnp.where(causal[rows], s_t, NEG_BIG)
                parts.append(s_t)
                shifts.append(slope * (((kt - i) * halves + d) * MOBA_BLOCK).astype(F32))
            m_new = ms[hh]
            for s_t, shift in zip(parts, shifts):
                m_new = jnp.maximum(m_new, col_max(s_t) + shift)
            p = jnp.concatenate([jnp.exp2(s_t - (m_new - shift)) for s_t, shift in zip(parts, shifts)],
                                axis=0).astype(BF16)
            acc_ref[hh] = acc_ref[hh] * jnp.exp2(ms[hh] - m_new) + _dot(vt_ref[hh, kt], p)
            new_ms.append(m_new)
        return tuple(new_ms)

    m0 = jnp.full((1, tq), 0.1 * NEG_BIG, F32)
    last_kt = i

    def run_tiles(first, count, ms, ends_sequence):
        bufs = (sa_ref, sb_ref)
        for j in range(count):
            if j + 1 < count or not ends_sequence:
                scores(first + j + 1, bufs[(j + 1) % 2])
            ms = step(first + j, bufs[j % 2], ms, ends_sequence and j + 1 == count)
        return ms

    scores(0, sa_ref)
    ms = lax.fori_loop(0, last_kt // TILES_PER_ITER,
                       lambda q, ms: run_tiles(q * TILES_PER_ITER, TILES_PER_ITER, ms, False), (m0, m0))
    rem = last_kt % TILES_PER_ITER
    for k in range(TILES_PER_ITER):
        @pl.when(rem == k)
        def _(k=k):
            run_tiles(last_kt - k, k + 1, ms, True)


    outs = []
    for hh in range(2):
        acc = acc_ref[hh]
        outs.append(acc[:HEAD_DIM] / acc[HEAD_DIM:HEAD_DIM + 1])
    o_ref[...] = jnp.transpose(jnp.concatenate(outs, axis=0)).astype(o_ref.dtype)


def _attention(qt, ka, vt):
    bsz, n_heads, _, s = qt.shape
    n_kt = ka.shape[2]
    tq = KV_TILE
    slopes = jnp.asarray([sl * LOG2E for sl in _alibi_slopes(n_heads)], F32)
    return pl.pallas_call(
        _attn_kernel,
        out_shape=jax.ShapeDtypeStruct((bsz, s, n_heads * HEAD_DIM), BF16),
        grid=(bsz, n_heads // 2, n_kt),
        in_specs=[pl.BlockSpec(memory_space=pltpu.SMEM),
                  pl.BlockSpec((None, 2, LANES, tq), lambda b, hp, i: (b, hp, 0, i)),
                  pl.BlockSpec((None, 2, n_kt, KV_TILE, LANES), lambda b, hp, i: (b, hp, 0, 0, 0)),
                  pl.BlockSpec((None, 2, n_kt, V_ROWS, KV_TILE), lambda b, hp, i: (b, hp, 0, 0, 0))],
        out_specs=pl.BlockSpec((None, tq, LANES), lambda b, hp, i: (b, i, hp)),
        scratch_shapes=[pltpu.VMEM((2, V_ROWS, tq), F32),
                        pltpu.VMEM((2, KV_TILE, tq), F32),
                        pltpu.VMEM((2, KV_TILE, tq), F32)],
        compiler_params=_cparams(("arbitrary", "arbitrary", "arbitrary")),
        name="attn",
    )(slopes, qt, ka, vt)


def _rms(y, g):
    return y * lax.rsqrt(jnp.mean(y * y, axis=-1, keepdims=True) + NORM_EPS) * g


def _merge_kernel(x_ref, z_ref, o_ref, sg_ref, mod_ref, wa_ref, wb_ref, wo_ref, wout_ref, gpost_ref, out_ref):
    d = x_ref.shape[1]
    z = z_ref[...]
    y_a = _dot(z, wa_ref[...]) * jax.nn.sigmoid(_dot(z, wb_ref[...]))
    y_b = _dot(o_ref[...], wo_ref[...])
    merged = sg_ref[:, :d].astype(F32) * y_a + sg_ref[:, d:].astype(F32) * y_b
    y = _dot(merged.astype(BF16), wout_ref[...])
    out_ref[...] = x_ref[...] + mod_ref[2:3, :] * _rms(y, gpost_ref[...])


def _merge(x, z_tm, o, sg, mod, wa, wb, wo, wout, g_post, tm):
    bsz, s, d = x.shape
    n_t = s // tm
    half_t = n_t // 2
    ssm_w = wa.shape[0]
    full = lambda a: pl.BlockSpec(a.shape, lambda b, i: (0,) * a.ndim)
    return pl.pallas_call(
        _merge_kernel,
        out_shape=jax.ShapeDtypeStruct((bsz, s, d), F32),
        grid=(bsz, n_t),
        in_specs=[pl.BlockSpec((None, tm, d), lambda b, i: (b, i, 0)),
                  pl.BlockSpec((tm, ssm_w), lambda b, i: (i % half_t, (i // half_t) * bsz + b)),
                  pl.BlockSpec((None, tm, o.shape[2]), lambda b, i: (b, i, 0)),
                  pl.BlockSpec((None, tm, sg.shape[2]), lambda b, i: (b, i, 0)),
                  pl.BlockSpec((None, N_MOD, d), lambda b, i: (b, 0, 0)),
                  full(wa), full(wb), full(wo), full(wout), full(g_post)],
        out_specs=pl.BlockSpec((None, tm, d), lambda b, i: (b, i, 0)),
        compiler_params=_cparams(("arbitrary", "arbitrary")),
        name="merge",
    )(x, z_tm, o, sg, mod, wa, wb, wo, wout, g_post)


def _ffn_kernel(x_ref, mod_ref, gpre_ref, wg_ref, wu_ref, wd_ref, gpost_ref, out_ref, *, n_chunks):
    x = x_ref[...]
    hn = _rms(x, gpre_ref[...]) * (1.0 + mod_ref[4:5, :]) + mod_ref[3:4, :]
    hb = hn.astype(BF16)
    ff = wg_ref.shape[1]
    cw = ff // n_chunks
    y = None
    for c in range(n_chunks):
        cols = slice(c * cw, (c + 1) * cw)
        gate = _dot(hb, wg_ref[:, cols])
        up = _dot(hb, wu_ref[:, cols])
        mid = (gate * jax.nn.sigmoid(gate) * up).astype(BF16)
        part = _dot(mid, wd_ref[cols, :])
        y = part if y is None else y + part
    out_ref[...] = x + mod_ref[5:6, :] * _rms(y, gpost_ref[...])


def _ffn(x, mod, g_pre, wg, wu, wd, g_post, tm):
    bsz, s, d = x.shape
    ff = wg.shape[1]
    n_chunks = 2 if (ff // 2) % LANES == 0 else 1
    full = lambda a: pl.BlockSpec(a.shape, lambda b, i: (0,) * a.ndim)
    return pl.pallas_call(
        functools.partial(_ffn_kernel, n_chunks=n_chunks),
        out_shape=jax.ShapeDtypeStruct((bsz, s, d), F32),
        grid=(bsz, s // tm),
        in_specs=[pl.BlockSpec((None, tm, d), lambda b, i: (b, i, 0)),
                  pl.BlockSpec((None, N_MOD, d), lambda b, i: (b, 0, 0)),
                  full(g_pre), full(wg), full(wu), full(wd), full(g_post)],
        out_specs=pl.BlockSpec((None, tm, d), lambda b, i: (b, i, 0)),
        compiler_params=_cparams(("arbitrary", "arbitrary")),
        name="ffn",
    )(x, mod, g_pre, wg, wu, wd, g_post)


def _layer(x, mod, g_pre_mix, g_post_mix, w_in, ssm_a_re, ssm_a_im, ssm_log_dt, ssm_b_re, ssm_b_im,
           ssm_c_re, ssm_c_im, ssm_d, w_glu_a, w_glu_b, w_attn_out, w_out,
           g_pre_ffn, g_post_ffn, w_ff_gate, w_ff_up, w_ff_down):
    bsz, s, d = x.shape
    n_groups, n_state = ssm_a_re.shape
    ssm_w = n_groups * SSM_GROUP
    width = w_attn_out.shape[0]
    n_heads = width // HEAD_DIM
    assert 2 * bsz == SUBLANES, "SSM scan packs 2 sequence halves x batch into the 8 sublanes"
    tm = min(512, s // 2)
    steps_per_tile = min(64, s // 2)
    assert s % (2 * tm) == 0 and tm % MOBA_BLOCK == 0 and (s // 2) % steps_per_tile == 0

    mod3 = mod.reshape(bsz, N_MOD, d)
    row = lambda g: g.reshape(1, -1)

    ab_re, ab_im, bbt_re, bbt_im = _ssm_prep(ssm_a_re, ssm_a_im, ssm_log_dt, ssm_b_re, ssm_b_im)
    wb = jnp.concatenate([_block_diag_tiles(bbt_re), _block_diag_tiles(bbt_im)], axis=-1).astype(BF16)
    wc = jnp.concatenate([_block_diag_tiles(jnp.transpose(ssm_c_re, (0, 2, 1))),
                          _block_diag_tiles(-jnp.transpose(ssm_c_im, (0, 2, 1)))], axis=1).astype(BF16)
    a_cat = jnp.stack([ab_re.reshape(-1), ab_im.reshape(-1)], axis=0)
    a_cat = jnp.broadcast_to(a_cat[:, None, :], (2, SUBLANES, a_cat.shape[1]))
    d_row = ssm_d.reshape(1, ssm_w)

    u_tm, qt, ka, vt, sg = _proj(x, mod3, row(g_pre_mix), w_in, n_heads, ssm_w, tm)

    zero_state = jnp.zeros((2, SUBLANES, n_groups * n_state), F32)
    first = _ssm_pass(u_tm, wb, a_cat, wc, d_row, zero_state, steps_per_tile, emit=False)
    init = jnp.concatenate([jnp.zeros_like(first[:, :bsz]), first[:, :bsz]], axis=1)
    z_tm, _ = _ssm_pass(u_tm, wb, a_cat, wc, d_row, init, steps_per_tile, emit=True)

    o = _attention(qt, ka, vt)

    x1 = _merge(x, z_tm, o, sg, mod3, w_glu_a.astype(BF16), w_glu_b.astype(BF16),
                w_attn_out.astype(BF16), w_out.astype(BF16), row(g_post_mix), tm)
    return _ffn(x1, mod3, row(g_pre_ffn), w_ff_gate.astype(BF16), w_ff_up.astype(BF16),
                w_ff_down.astype(BF16), row(g_post_ffn), tm)


def kernel(x, c, w_ada, b_ada, g_pre_mix, g_post_mix, w_in, ssm_a_re, ssm_a_im, ssm_log_dt, ssm_b_re, ssm_b_im, ssm_c_re, ssm_c_im, ssm_d, w_glu_a, w_glu_b, w_attn_out, w_out, g_pre_ffn, g_post_ffn, w_ff_gate, w_ff_up, w_ff_down):
    depth = w_ada.shape[0]
    for l in range(depth):
        mod = _adaln(c, w_ada[l], b_ada[l])
        x = _layer(x, mod, g_pre_mix[l], g_post_mix[l], w_in[l], ssm_a_re[l], ssm_a_im[l], ssm_log_dt[l],
                   ssm_b_re[l], ssm_b_im[l], ssm_c_re[l], ssm_c_im[l], ssm_d[l], w_glu_a[l], w_glu_b[l],
                   w_attn_out[l], w_out[l], g_pre_ffn[l], g_post_ffn[l], w_ff_gate[l], w_ff_up[l], w_ff_down[l])
    return x
```

```python
import functools
import math

import jax
import jax.numpy as jnp
from jax import lax
from jax.experimental import pallas as pl
from jax.experimental.pallas import tpu as pltpu

F32 = jnp.float32
BF16 = jnp.bfloat16

SSM_GROUP = 16
SSM_STATE = 64
HEAD_DIM = 64
MOBA_BLOCK = 256
MOBA_TOPK = 3
N_MOD = 6
NORM_EPS = 1e-6
LANES = 128
SUBLANES = 8
GROUPS_PER_TILE = LANES // SSM_GROUP
NEG_BIG = -1e30
LOG2E = math.log2(math.e)
MASK_FEATS = 32
V_ROWS = HEAD_DIM + 16
KV_TILE = 2 * MOBA_BLOCK
TILES_PER_ITER = 8
VMEM_LIMIT = 56 * 1024 * 1024


def _cparams(sem):
    return pltpu.CompilerParams(dimension_semantics=sem, vmem_limit_bytes=VMEM_LIMIT)


def _dot(a, b):
    return jnp.dot(a, b, preferred_element_type=F32)


def _dot_split3(a, b):
    a_hi = a.astype(BF16)
    b_hi = b.astype(BF16)
    a_lo = (a - a_hi.astype(F32)).astype(BF16)
    b_lo = (b - b_hi.astype(F32)).astype(BF16)
    return _dot(a_hi, b_hi) + (_dot(a_hi, b_lo) + _dot(a_lo, b_hi))


def _dot_nt(a, b):
    return lax.dot_general(a, b, (((1,), (1,)), ((), ())), preferred_element_type=F32)


def _adaln_kernel(c_ref, w_ref, b_ref, o_ref):
    c = c_ref[...]
    act = c * jax.nn.sigmoid(c)
    o_ref[...] = jnp.dot(act, w_ref[...], preferred_element_type=F32,
                         precision=lax.Precision.HIGHEST) + b_ref[...]


def _adaln(c, w_ada, b_ada):
    bsz, d = c.shape
    n = w_ada.shape[1]
    tn = n // N_MOD
    return pl.pallas_call(
        _adaln_kernel,
        out_shape=jax.ShapeDtypeStruct((bsz, n), F32),
        grid=(n // tn,),
        in_specs=[pl.BlockSpec((bsz, d), lambda j: (0, 0)),
                  pl.BlockSpec((d, tn), lambda j: (0, j)),
                  pl.BlockSpec((1, tn), lambda j: (0, j))],
        out_specs=pl.BlockSpec((bsz, tn), lambda j: (0, j)),
        compiler_params=_cparams(("arbitrary",)),
        name="adaln",
    )(c, w_ada, b_ada.reshape(1, n))


def _ssm_prep_kernel(are_ref, aim_ref, ldt_ref, bre_ref, bim_ref,
                     abre_ref, abim_ref, bbre_ref, bbim_ref):
    a_re = are_ref[...]
    a_im = aim_ref[...]
    dt = jnp.exp(ldt_ref[...])
    mag = jnp.exp(dt * a_re)
    ab_re = mag * jnp.cos(dt * a_im)
    ab_im = mag * jnp.sin(dt * a_im)
    den = a_re * a_re + a_im * a_im
    nr = ab_re - 1.0
    co_re = (nr * a_re + ab_im * a_im) / den
    co_im = (ab_im * a_re - nr * a_im) / den
    abre_ref[...] = ab_re
    abim_ref[...] = ab_im
    b_re = bre_ref[...]
    b_im = bim_ref[...]
    bbre_ref[...] = co_re[:, None, :] * b_re - co_im[:, None, :] * b_im
    bbim_ref[...] = co_re[:, None, :] * b_im + co_im[:, None, :] * b_re


def _ssm_prep(a_re, a_im, log_dt, b_re, b_im):
    g, p = a_re.shape
    c = b_re.shape[-1]
    bt_re = jnp.transpose(b_re, (0, 2, 1))
    bt_im = jnp.transpose(b_im, (0, 2, 1))
    return pl.pallas_call(
        _ssm_prep_kernel,
        out_shape=(jax.ShapeDtypeStruct((g, p), F32), jax.ShapeDtypeStruct((g, p), F32),
                   jax.ShapeDtypeStruct((g, c, p), F32), jax.ShapeDtypeStruct((g, c, p), F32)),
        name="ssm_prep",
    )(a_re, a_im, log_dt.reshape(g, 1), bt_re, bt_im)


def _block_diag_tiles(w):
    g, r, cc = w.shape
    nt = g // GROUPS_PER_TILE
    w = w.reshape(nt, GROUPS_PER_TILE, r, cc)
    eye = jnp.eye(GROUPS_PER_TILE, dtype=w.dtype)
    return jnp.einsum('sgrc,gh->sgrhc', w, eye).reshape(nt, GROUPS_PER_TILE * r, GROUPS_PER_TILE * cc)


def _alibi_slopes(n_heads):
    return [2.0 ** (-8.0 * (h + 1) / n_heads) for h in range(n_heads)]


def _proj_kernel(x_ref, mod_ref, gpre_ref, wu_ref, wqt_ref, wk_ref, wvt_ref, wg_ref, perm_ref, kbias_ref,
                 u_ref, qt_ref, ka_ref, vt_ref, sg_ref, kmt_ref, *, n_heads, n_blk):
    i = pl.program_id(1)
    tm = x_ref.shape[0]
    width = n_heads * HEAD_DIM
    blk_per_tile = tm // MOBA_BLOCK

    @pl.when(i == 0)
    def _():
        kmt_ref[...] = jnp.zeros_like(kmt_ref)

    x = x_ref[...]
    ms = jnp.mean(x * x, axis=-1, keepdims=True)
    gain = gpre_ref[...] * (1.0 + mod_ref[1:2, :])
    hn = x * lax.rsqrt(ms + NORM_EPS) * gain + mod_ref[0:1, :]
    hb = hn.astype(BF16)

    q_t = _dot_nt(wqt_ref[...], hb)
    k = _dot(hb, wk_ref[...])

    head_of_lane = lax.broadcasted_iota(jnp.int32, (n_heads, width), 1) // HEAD_DIM
    head_mask = (head_of_lane == lax.broadcasted_iota(jnp.int32, (n_heads, width), 0)).astype(F32)
    for lb in range(blk_per_tile):
        km = jnp.mean(k[lb * MOBA_BLOCK:(lb + 1) * MOBA_BLOCK, :], axis=0, keepdims=True)
        row0 = pl.multiple_of((i * blk_per_tile + lb) * n_heads, n_heads)
        kmt_ref[pl.ds(row0, n_heads), :] = km * head_mask

    gate = _dot_split3(kmt_ref[...], q_t)
    u_ref[...] = _dot(hb, wu_ref[...]).astype(BF16)
    v_t = _dot_nt(wvt_ref[...], hb)
    gate = gate.reshape(n_blk, n_heads, tm)
    n_iota = lax.broadcasted_iota(jnp.int32, (n_blk, n_heads, tm), 0)
    own = i * blk_per_tile + lax.broadcasted_iota(jnp.int32, (n_blk, n_heads, tm), 2) // MOBA_BLOCK
    neg_inf = jnp.float32(-jnp.inf)
    cur = jnp.where(n_iota < own, gate, neg_inf)
    sel = jnp.zeros((n_blk, n_heads, tm), F32)
    for _ in range(MOBA_TOPK):
        best = jnp.max(cur, axis=0, keepdims=True)
        idx = jnp.min(jnp.where(cur == best, n_iota, n_blk), axis=0, keepdims=True)
        hit = (n_iota == idx) & (best > neg_inf)
        sel = jnp.where(hit, 1.0, sel)
        cur = jnp.where(hit, neg_inf, cur)
    unsel = 1.0 - sel - (n_iota == own).astype(F32)
    unsel = unsel.reshape(n_blk * n_heads, tm).astype(BF16)
    mask_rows = _dot(perm_ref[...], unsel) * NEG_BIG
    sg_ref[...] = jax.nn.sigmoid(_dot(hb, wg_ref[...])).astype(BF16)

    sub8 = lax.broadcasted_iota(jnp.int32, (SUBLANES, tm), 0)
    ones3 = (sub8 < 3).astype(F32)
    ones1 = (sub8 < 1).astype(F32)
    zeros = lambda r: jnp.zeros((r, tm), F32)
    lane = lax.broadcasted_iota(jnp.int32, (tm, LANES), 1)
    row = lax.broadcasted_iota(jnp.int32, (tm, LANES), 0)
    own_row = i * blk_per_tile + row // MOBA_BLOCK
    scale = HEAD_DIM ** -0.5 * LOG2E
    slopes = _alibi_slopes(n_heads)
    pad_feats = LANES - HEAD_DIM - MASK_FEATS - SUBLANES
    for h in range(n_heads):
        rows = slice(h * HEAD_DIM, (h + 1) * HEAD_DIM)
        tile = slice((h // 2) * LANES, (h // 2 + 1) * LANES)
        q_rows = q_t[rows] * scale
        m_rows = mask_rows[h * n_blk:(h + 1) * n_blk]
        if n_blk < MASK_FEATS:
            m_rows = jnp.concatenate([m_rows, zeros(MASK_FEATS - n_blk)], axis=0)
        if h % 2 == 0:
            qt = jnp.concatenate([q_rows, m_rows, ones3, zeros(pad_feats)], axis=0)
            data_lanes = lane < HEAD_DIM
            feat0 = HEAD_DIM
        else:
            qt = jnp.concatenate([m_rows, ones3, zeros(pad_feats), q_rows], axis=0)
            data_lanes = lane >= HEAD_DIM
            feat0 = 0
        qt_ref[h] = qt.astype(BF16)
        feats = jnp.where(lane == feat0 + own_row, 1.0, kbias_ref[h])
        ka = jnp.where(data_lanes, k[:, tile], feats).astype(BF16)
        vt = jnp.concatenate([v_t[rows], ones1, zeros(V_ROWS - HEAD_DIM - SUBLANES)], axis=0).astype(BF16)
        for kt in range(tm // KV_TILE):
            ka_ref[h, kt] = ka[kt * KV_TILE:(kt + 1) * KV_TILE]
            vt_ref[h, kt] = vt[:, kt * KV_TILE:(kt + 1) * KV_TILE]


def _leading_8_bits(x):
    c = x * jnp.float32(2 ** 16 + 1)
    return c - (c - x)


def _alibi_key_features(n_heads, tm):
    pos = (jnp.arange(tm) % MOBA_BLOCK).astype(F32)
    lane = jnp.arange(LANES)
    out = []
    for h, slope in enumerate(_alibi_slopes(n_heads)):
        feat0 = (HEAD_DIM if h % 2 == 0 else 0) + MASK_FEATS
        bias = (slope * LOG2E) * pos
        hi = _leading_8_bits(bias)
        mid = _leading_8_bits(bias - hi)
        lo = bias - hi - mid
        feats = jnp.zeros((tm, LANES), F32)
        for j, term in enumerate((hi, mid, lo)):
            feats = jnp.where(lane[None, :] == feat0 + j, term[:, None], feats)
        out.append(feats)
    return jnp.stack(out, axis=0)


def _row_permutation(n_blk, n_heads):
    r = jnp.arange(n_heads * n_blk)
    src = (r % n_blk) * n_heads + r // n_blk
    return (src[:, None] == jnp.arange(n_blk * n_heads)[None, :]).astype(BF16)


def _proj(x, mod, g_pre, w_in, n_heads, ssm_w, tm):
    bsz, s, d = x.shape
    n_t = s // tm
    half_t = n_t // 2
    n_blk = s // MOBA_BLOCK
    width = n_heads * HEAD_DIM
    assert n_blk <= MASK_FEATS and tm % KV_TILE == 0
    c = [0, ssm_w, ssm_w + width, ssm_w + 2 * width, ssm_w + 3 * width, w_in.shape[1]]
    gate_w = c[5] - c[4]
    wu = w_in[:, c[0]:c[1]].astype(BF16)
    wqt = w_in[:, c[1]:c[2]].T.astype(BF16)
    wk = w_in[:, c[2]:c[3]].astype(BF16)
    wvt = w_in[:, c[3]:c[4]].T.astype(BF16)
    wg = w_in[:, c[4]:c[5]].astype(BF16)
    perm = _row_permutation(n_blk, n_heads)
    kbias = _alibi_key_features(n_heads, tm)
    full = lambda a: pl.BlockSpec(a.shape, lambda b, i: (0,) * a.ndim)
    kern = functools.partial(_proj_kernel, n_heads=n_heads, n_blk=n_blk)
    return pl.pallas_call(
        kern,
        out_shape=(jax.ShapeDtypeStruct((s // 2, 2 * bsz * ssm_w), BF16),
                   jax.ShapeDtypeStruct((bsz, n_heads, LANES, s), BF16),
                   jax.ShapeDtypeStruct((bsz, n_heads, s // KV_TILE, KV_TILE, LANES), BF16),
                   jax.ShapeDtypeStruct((bsz, n_heads, s // KV_TILE, V_ROWS, KV_TILE), BF16),
                   jax.ShapeDtypeStruct((bsz, s, gate_w), BF16)),
        grid=(bsz, n_t),
        in_specs=[pl.BlockSpec((None, tm, d), lambda b, i: (b, i, 0)),
                  pl.BlockSpec((None, N_MOD, d), lambda b, i: (b, 0, 0)),
                  full(g_pre), full(wu), full(wqt), full(wk), full(wvt), full(wg), full(perm), full(kbias)],
        out_specs=(pl.BlockSpec((tm, ssm_w), lambda b, i: (i % half_t, (i // half_t) * bsz + b)),
                   pl.BlockSpec((None, n_heads, LANES, tm), lambda b, i: (b, 0, 0, i)),
                   pl.BlockSpec((None, n_heads, tm // KV_TILE, KV_TILE, LANES), lambda b, i: (b, 0, i, 0, 0)),
                   pl.BlockSpec((None, n_heads, tm // KV_TILE, V_ROWS, KV_TILE), lambda b, i: (b, 0, i, 0, 0)),
                   pl.BlockSpec((None, tm, gate_w), lambda b, i: (b, i, 0))),
        scratch_shapes=[pltpu.VMEM((n_blk * n_heads, width), F32)],
        compiler_params=_cparams(("arbitrary", "arbitrary")),
        name="proj",
    )(x, mod, g_pre, wu, wqt, wk, wvt, wg, perm, kbias)


def _gelu_tanh(x):
    c = math.sqrt(2.0 / math.pi)
    return 0.5 * x * (1.0 + jnp.tanh(c * (x + 0.044715 * (x * x * x))))


def _ssm_kernel(u_ref, un_ref, perm_ref, permt_ref, wb_ref, a_ref, wc_ref, d_ref, init_ref, *rest, emit):
    if emit:
        z_ref, fin_ref, bua_ref, bub_ref, usa_ref, usb_ref, st_ref, zs_ref = rest
    else:
        fin_ref, bua_ref, bub_ref, usa_ref, usb_ref, st_ref = rest
    i = pl.program_id(0)
    steps = un_ref.shape[0]
    ssm_w = u_ref.shape[1] // SUBLANES
    n_tiles = wb_ref.shape[0]
    st_w = wb_ref.shape[2] // 2

    def project(src_ref, t0, bu_ref, us_ref):
        u_nat = jnp.concatenate([src_ref[t0:t0 + steps, s * ssm_w:(s + 1) * ssm_w]
                                 for s in range(SUBLANES)], axis=0)
        us_ref[...] = _dot(perm_ref[...], u_nat).astype(BF16)
        for sg in range(n_tiles):
            r = _dot(us_ref[:, sg * LANES:(sg + 1) * LANES], wb_ref[sg])
            bu_ref[0, :, sg * st_w:(sg + 1) * st_w] = r[:, :st_w]
            bu_ref[1, :, sg * st_w:(sg + 1) * st_w] = r[:, st_w:]

    def scan(bu_ref, x_re, x_im):
        for t in range(steps):
            rows = slice(t * SUBLANES, (t + 1) * SUBLANES)
            a_re = a_ref[0]
            a_im = a_ref[1]
            n_re = a_re * x_re - a_im * x_im + bu_ref[0, rows, :]
            n_im = a_re * x_im + a_im * x_re + bu_ref[1, rows, :]
            bu_ref[0, rows, :] = n_re
            bu_ref[1, rows, :] = n_im
            x_re, x_im = n_re, n_im
        return x_re, x_im

    def output(bu_ref, us_ref, t0):
        for sg in range(n_tiles):
            xr = bu_ref[0, :, sg * st_w:(sg + 1) * st_w].astype(BF16)
            xi = bu_ref[1, :, sg * st_w:(sg + 1) * st_w].astype(BF16)
            y = _dot(xr, wc_ref[sg, :st_w, :]) + _dot(xi, wc_ref[sg, st_w:, :])
            lanes = slice(sg * LANES, (sg + 1) * LANES)
            y = y + d_ref[:, lanes] * us_ref[:, lanes].astype(F32)
            zs_ref[:, lanes] = _gelu_tanh(y).astype(BF16)
        z_nat = _dot(permt_ref[...], zs_ref[...]).astype(BF16)
        for s in range(SUBLANES):
            z_ref[t0:t0 + steps, s * ssm_w:(s + 1) * ssm_w] = z_nat[s * steps:(s + 1) * steps]

    @pl.when(i == 0)
    def _():
        st_ref[...] = init_ref[...]
        project(u_ref, 0, bua_ref, usa_ref)

    project(u_ref, steps, bub_ref, usb_ref)
    x_re, x_im = scan(bua_ref, st_ref[0], st_ref[1])
    if emit:
        output(bua_ref, usa_ref, 0)
    project(un_ref, 0, bua_ref, usa_ref)
    x_re, x_im = scan(bub_ref, x_re, x_im)
    if emit:
        output(bub_ref, usb_ref, steps)
    st_ref[0] = x_re
    st_ref[1] = x_im
    fin_ref[0] = x_re
    fin_ref[1] = x_im


def _ssm_pass(u_tm, wb, a_cat, wc, d_row, init, steps_per_tile, emit):
    n_steps, slot_w = u_tm.shape
    ssm_w = slot_w // SUBLANES
    rows = steps_per_tile * SUBLANES
    n_tiles = n_steps // steps_per_tile
    assert n_tiles % 2 == 0
    st_lanes = a_cat.shape[2]
    r = jnp.arange(rows)
    src = (r % SUBLANES) * steps_per_tile + r // SUBLANES
    perm = (src[:, None] == r[None, :]).astype(BF16)
    perm_t = perm.T
    fin_shape = jax.ShapeDtypeStruct((2, SUBLANES, st_lanes), F32)
    fin_spec = pl.BlockSpec((2, SUBLANES, st_lanes), lambda i: (0, 0, 0))
    scratch = [pltpu.VMEM((2, rows, st_lanes), F32), pltpu.VMEM((2, rows, st_lanes), F32),
               pltpu.VMEM((rows, ssm_w), BF16), pltpu.VMEM((rows, ssm_w), BF16),
               pltpu.VMEM((2, SUBLANES, st_lanes), F32)]
    if emit:
        out_shape = (jax.ShapeDtypeStruct((n_steps, slot_w), BF16), fin_shape)
        out_specs = (pl.BlockSpec((2 * steps_per_tile, slot_w), lambda i: (i, 0)), fin_spec)
        scratch.append(pltpu.VMEM((rows, ssm_w), BF16))
    else:
        out_shape = fin_shape
        out_specs = fin_spec
    return pl.pallas_call(
        functools.partial(_ssm_kernel, emit=emit),
        out_shape=out_shape,
        grid=(n_tiles // 2,),
        in_specs=[pl.BlockSpec((2 * steps_per_tile, slot_w), lambda i: (i, 0)),
                  pl.BlockSpec((steps_per_tile, slot_w), lambda i: (jnp.minimum(2 * i + 2, n_tiles - 1), 0)),
                  pl.BlockSpec(perm.shape, lambda i: (0, 0)),
                  pl.BlockSpec(perm.shape, lambda i: (0, 0)),
                  pl.BlockSpec(wb.shape, lambda i: (0, 0, 0)),
                  pl.BlockSpec(a_cat.shape, lambda i: (0, 0, 0)),
                  pl.BlockSpec(wc.shape, lambda i: (0, 0, 0)),
                  pl.BlockSpec(d_row.shape, lambda i: (0, 0)),
                  pl.BlockSpec(init.shape, lambda i: (0, 0, 0))],
        out_specs=out_specs,
        scratch_shapes=scratch,
        compiler_params=_cparams(("arbitrary",)),
        name="ssm_scan" if emit else "ssm_state",
    )(u_tm, u_tm, perm, perm_t, wb, a_cat, wc, d_row, init)


def _attn_kernel(slope_ref, qt_ref, ka_ref, vt_ref, o_ref, acc_ref, sa_ref, sb_ref):
    hp = pl.program_id(1)
    i = pl.program_id(2)
    tq = qt_ref.shape[2]
    key_row = lax.broadcasted_iota(jnp.int32, (KV_TILE, 1), 0)
    q_lane = lax.broadcasted_iota(jnp.int32, (1, tq), 1)
    causal = key_row <= q_lane
    acc_ref[...] = jnp.zeros_like(acc_ref)
    halves = KV_TILE // MOBA_BLOCK

    def scores(kt, s_ref):
        for hh in range(2):
            s_ref[hh] = _dot(ka_ref[hh, kt], qt_ref[hh])

    def col_max(s_t):
        rows = s_t.shape[0]
        while rows > SUBLANES:
            rows //= 2
            s_t = jnp.maximum(s_t[:rows], s_t[rows:])
        return jnp.max(s_t, axis=0, keepdims=True)

    def step(kt, s_ref, ms, last):
        new_ms = []
        for hh in range(2):
            slope = slope_ref[hp * 2 + hh]
            parts, shifts = [], []
            for d in range(halves):
                rows = slice(d * MOBA_BLOCK, (d + 1) * MOBA_BLOCK)
                s_t = s_ref[hh, rows, :]
                if last:
                    s_t = jnp.where(causal[rows], s_t, NEG_BIG)
                parts.append(s_t)
                shifts.append(slope * (((kt - i) * halves + d) * MOBA_BLOCK).astype(F32))
            m_new = ms[hh]
            for s_t, shift in zip(parts, shifts):
                m_new = jnp.maximum(m_new, col_max(s_t) + shift)
            p = jnp.concatenate([jnp.exp2(s_t - (m_new - shift)) for s_t, shift in zip(parts, shifts)],
                                axis=0).astype(BF16)
            acc_ref[hh] = acc_ref[hh] * jnp.exp2(ms[hh] - m_new) + _dot(vt_ref[hh, kt], p)
            new_ms.append(m_new)
        return tuple(new_ms)

    m0 = jnp.full((1, tq), 0.1 * NEG_BIG, F32)
    last_kt = i

    def run_tiles(first, count, ms, ends_sequence):
        bufs = (sa_ref, sb_ref)
        for j in range(count):
            if j + 1 < count or not ends_sequence:
                scores(first + j + 1, bufs[(j + 1) % 2])
            ms = step(first + j, bufs[j % 2], ms, ends_sequence and j + 1 == count)
        return ms

    scores(0, sa_ref)
    ms = lax.fori_loop(0, last_kt // TILES_PER_ITER,
                       lambda q, ms: run_tiles(q * TILES_PER_ITER, TILES_PER_ITER, ms, False), (m0, m0))
    rem = last_kt % TILES_PER_ITER
    for k in range(TILES_PER_ITER):
        @pl.when(rem == k)
        def _(k=k):
            run_tiles(last_kt - k, k + 1, ms, True)


    outs = []
    for hh in range(2):
        acc = acc_ref[hh]
        outs.append(acc[:HEAD_DIM] / acc[HEAD_DIM:HEAD_DIM + 1])
    o_ref[...] = jnp.transpose(jnp.concatenate(outs, axis=0)).astype(o_ref.dtype)


def _attention(qt, ka, vt):
    bsz, n_heads, _, s = qt.shape
    n_kt = ka.shape[2]
    tq = KV_TILE
    slopes = jnp.asarray([sl * LOG2E for sl in _alibi_slopes(n_heads)], F32)
    return pl.pallas_call(
        _attn_kernel,
        out_shape=jax.ShapeDtypeStruct((bsz, s, n_heads * HEAD_DIM), BF16),
        grid=(bsz, n_heads // 2, n_kt),
        in_specs=[pl.BlockSpec(memory_space=pltpu.SMEM),
                  pl.BlockSpec((None, 2, LANES, tq), lambda b, hp, i: (b, hp, 0, i)),
                  pl.BlockSpec((None, 2, n_kt, KV_TILE, LANES), lambda b, hp, i: (b, hp, 0, 0, 0)),
                  pl.BlockSpec((None, 2, n_kt, V_ROWS, KV_TILE), lambda b, hp, i: (b, hp, 0, 0, 0))],
        out_specs=pl.BlockSpec((None, tq, LANES), lambda b, hp, i: (b, i, hp)),
        scratch_shapes=[pltpu.VMEM((2, V_ROWS, tq), F32),
                        pltpu.VMEM((2, KV_TILE, tq), F32),
                        pltpu.VMEM((2, KV_TILE, tq), F32)],
        compiler_params=_cparams(("arbitrary", "arbitrary", "arbitrary")),
        name="attn",
    )(slopes, qt, ka, vt)


def _rms(y, g):
    return y * lax.rsqrt(jnp.mean(y * y, axis=-1, keepdims=True) + NORM_EPS) * g


def _merge_kernel(x_ref, z_ref, o_ref, sg_ref, mod_ref, wa_ref, wb_ref, wo_ref, wout_ref, gpost_ref, out_ref):
    d = x_ref.shape[1]
    z = z_ref[...]
    y_a = _dot(z, wa_ref[...]) * jax.nn.sigmoid(_dot(z, wb_ref[...]))
    y_b = _dot(o_ref[...], wo_ref[...])
    merged = sg_ref[:, :d].astype(F32) * y_a + sg_ref[:, d:].astype(F32) * y_b
    y = _dot(merged.astype(BF16), wout_ref[...])
    out_ref[...] = x_ref[...] + mod_ref[2:3, :] * _rms(y, gpost_ref[...])


def _merge(x, z_tm, o, sg, mod, wa, wb, wo, wout, g_post, tm):
    bsz, s, d = x.shape
    n_t = s // tm
    half_t = n_t // 2
    ssm_w = wa.shape[0]
    full = lambda a: pl.BlockSpec(a.shape, lambda b, i: (0,) * a.ndim)
    return pl.pallas_call(
        _merge_kernel,
        out_shape=jax.ShapeDtypeStruct((bsz, s, d), F32),
        grid=(bsz, n_t),
        in_specs=[pl.BlockSpec((None, tm, d), lambda b, i: (b, i, 0)),
                  pl.BlockSpec((tm, ssm_w), lambda b, i: (i % half_t, (i // half_t) * bsz + b)),
                  pl.BlockSpec((None, tm, o.shape[2]), lambda b, i: (b, i, 0)),
                  pl.BlockSpec((None, tm, sg.shape[2]), lambda b, i: (b, i, 0)),
                  pl.BlockSpec((None, N_MOD, d), lambda b, i: (b, 0, 0)),
                  full(wa), full(wb), full(wo), full(wout), full(g_post)],
        out_specs=pl.BlockSpec((None, tm, d), lambda b, i: (b, i, 0)),
        compiler_params=_cparams(("arbitrary", "arbitrary")),
        name="merge",
    )(x, z_tm, o, sg, mod, wa, wb, wo, wout, g_post)


def _ffn_kernel(x_ref, mod_ref, gpre_ref, wg_ref, wu_ref, wd_ref, gpost_ref, out_ref, *, n_chunks):
    x = x_ref[...]
    hn = _rms(x, gpre_ref[...]) * (1.0 + mod_ref[4:5, :]) + mod_ref[3:4, :]
    hb = hn.astype(BF16)
    ff = wg_ref.shape[1]
    cw = ff // n_chunks
    y = None
    for c in range(n_chunks):
        cols = slice(c * cw, (c + 1) * cw)
        gate = _dot(hb, wg_ref[:, cols])
        up = _dot(hb, wu_ref[:, cols])
        mid = (gate * jax.nn.sigmoid(gate) * up).astype(BF16)
        part = _dot(mid, wd_ref[cols, :])
        y = part if y is None else y + part
    out_ref[...] = x + mod_ref[5:6, :] * _rms(y, gpost_ref[...])


def _ffn(x, mod, g_pre, wg, wu, wd, g_post, tm):
    bsz, s, d = x.shape
    ff = wg.shape[1]
    n_chunks = 2 if (ff // 2) % LANES == 0 else 1
    full = lambda a: pl.BlockSpec(a.shape, lambda b, i: (0,) * a.ndim)
    return pl.pallas_call(
        functools.partial(_ffn_kernel, n_chunks=n_chunks),
        out_shape=jax.ShapeDtypeStruct((bsz, s, d), F32),
        grid=(bsz, s // tm),
        in_specs=[pl.BlockSpec((None, tm, d), lambda b, i: (b, i, 0)),
                  pl.BlockSpec((None, N_MOD, d), lambda b, i: (b, 0, 0)),
                  full(g_pre), full(wg), full(wu), full(wd), full(g_post)],
        out_specs=pl.BlockSpec((None, tm, d), lambda b, i: (b, i, 0)),
        compiler_params=_cparams(("arbitrary", "arbitrary")),
        name="ffn",
    )(x, mod, g_pre, wg, wu, wd, g_post)


def _layer(x, mod, g_pre_mix, g_post_mix, w_in, ssm_a_re, ssm_a_im, ssm_log_dt, ssm_b_re, ssm_b_im,
           ssm_c_re, ssm_c_im, ssm_d, w_glu_a, w_glu_b, w_attn_out, w_out,
           g_pre_ffn, g_post_ffn, w_ff_gate, w_ff_up, w_ff_down):
    bsz, s, d = x.shape
    n_groups, n_state = ssm_a_re.shape
    ssm_w = n_groups * SSM_GROUP
    width = w_attn_out.shape[0]
    n_heads = width // HEAD_DIM
    assert 2 * bsz == SUBLANES, "SSM scan packs 2 sequence halves x batch into the 8 sublanes"
    tm = min(512, s // 2)
    steps_per_tile = min(64, s // 2)
    assert s % (2 * tm) == 0 and tm % MOBA_BLOCK == 0 and (s // 2) % steps_per_tile == 0

    mod3 = mod.reshape(bsz, N_MOD, d)
    row = lambda g: g.reshape(1, -1)

    ab_re, ab_im, bbt_re, bbt_im = _ssm_prep(ssm_a_re, ssm_a_im, ssm_log_dt, ssm_b_re, ssm_b_im)
    wb = jnp.concatenate([_block_diag_tiles(bbt_re), _block_diag_tiles(bbt_im)], axis=-1).astype(BF16)
    wc = jnp.concatenate([_block_diag_tiles(jnp.transpose(ssm_c_re, (0, 2, 1))),
                          _block_diag_tiles(-jnp.transpose(ssm_c_im, (0, 2, 1)))], axis=1).astype(BF16)
    a_cat = jnp.stack([ab_re.reshape(-1), ab_im.reshape(-1)], axis=0)
    a_cat = jnp.broadcast_to(a_cat[:, None, :], (2, SUBLANES, a_cat.shape[1]))
    d_row = ssm_d.reshape(1, ssm_w)

    u_tm, qt, ka, vt, sg = _proj(x, mod3, row(g_pre_mix), w_in, n_heads, ssm_w, tm)

    zero_state = jnp.zeros((2, SUBLANES, n_groups * n_state), F32)
    first = _ssm_pass(u_tm, wb, a_cat, wc, d_row, zero_state, steps_per_tile, emit=False)
    init = jnp.concatenate([jnp.zeros_like(first[:, :bsz]), first[:, :bsz]], axis=1)
    z_tm, _ = _ssm_pass(u_tm, wb, a_cat, wc, d_row, init, steps_per_tile, emit=True)

    o = _attention(qt, ka, vt)

    x1 = _merge(x, z_tm, o, sg, mod3, w_glu_a.astype(BF16), w_glu_b.astype(BF16),
                w_attn_out.astype(BF16), w_out.astype(BF16), row(g_post_mix), tm)
    return _ffn(x1, mod3, row(g_pre_ffn), w_ff_gate.astype(BF16), w_ff_up.astype(BF16),
                w_ff_down.astype(BF16), row(g_post_ffn), tm)


def kernel(x, c, w_ada, b_ada, g_pre_mix, g_post_mix, w_in, ssm_a_re, ssm_a_im, ssm_log_dt, ssm_b_re, ssm_b_im, ssm_c_re, ssm_c_im, ssm_d, w_glu_a, w_glu_b, w_attn_out, w_out, g_pre_ffn, g_post_ffn, w_ff_gate, w_ff_up, w_ff_down):
    depth = w_ada.shape[0]
    for l in range(depth):
        mod = _adaln(c, w_ada[l], b_ada[l])
        x = _layer(x, mod, g_pre_mix[l], g_post_mix[l], w_in[l], ssm_a_re[l], ssm_a_im[l], ssm_log_dt[l],
                   ssm_b_re[l], ssm_b_im[l], ssm_c_re[l], ssm_c_im[l], ssm_d[l], w_glu_a[l], w_glu_b[l],
                   w_attn_out[l], w_out[l], g_pre_ffn[l], g_post_ffn[l], w_ff_gate[l], w_ff_up[l], w_ff_down[l])
    return x
```

```python
import functools
import math

import jax
import jax.numpy as jnp
from jax import lax
from jax.experimental import pallas as pl
from jax.experimental.pallas import tpu as pltpu

F32 = jnp.float32
BF16 = jnp.bfloat16

SSM_GROUP = 16
SSM_STATE = 64
HEAD_DIM = 64
MOBA_BLOCK = 256
MOBA_TOPK = 3
N_MOD = 6
NORM_EPS = 1e-6
LANES = 128
SUBLANES = 8
GROUPS_PER_TILE = LANES // SSM_GROUP
NEG_BIG = -1e30
LOG2E = math.log2(math.e)
MASK_FEATS = 32
V_ROWS = HEAD_DIM + 16
KV_TILE = 2 * MOBA_BLOCK
TILES_PER_ITER = 8
VMEM_LIMIT = 56 * 1024 * 1024


def _cparams(sem):
    return pltpu.CompilerParams(dimension_semantics=sem, vmem_limit_bytes=VMEM_LIMIT)


def _dot(a, b):
    return jnp.dot(a, b, preferred_element_type=F32)


def _dot_split3(a, b):
    a_hi = a.astype(BF16)
    b_hi = b.astype(BF16)
    a_lo = (a - a_hi.astype(F32)).astype(BF16)
    b_lo = (b - b_hi.astype(F32)).astype(BF16)
    return _dot(a_hi, b_hi) + (_dot(a_hi, b_lo) + _dot(a_lo, b_hi))


def _dot_nt(a, b):
    return lax.dot_general(a, b, (((1,), (1,)), ((), ())), preferred_element_type=F32)


def _adaln_kernel(c_ref, w_ref, b_ref, o_ref):
    c = c_ref[...]
    act = c * jax.nn.sigmoid(c)
    o_ref[...] = jnp.dot(act, w_ref[...], preferred_element_type=F32,
                         precision=lax.Precision.HIGHEST) + b_ref[...]


def _adaln(c, w_ada, b_ada):
    bsz, d = c.shape
    n = w_ada.shape[1]
    tn = n // N_MOD
    return pl.pallas_call(
        _adaln_kernel,
        out_shape=jax.ShapeDtypeStruct((bsz, n), F32),
        grid=(n // tn,),
        in_specs=[pl.BlockSpec((bsz, d), lambda j: (0, 0)),
                  pl.BlockSpec((d, tn), lambda j: (0, j)),
                  pl.BlockSpec((1, tn), lambda j: (0, j))],
        out_specs=pl.BlockSpec((bsz, tn), lambda j: (0, j)),
        compiler_params=_cparams(("arbitrary",)),
        name="adaln",
    )(c, w_ada, b_ada.reshape(1, n))


def _ssm_prep_kernel(are_ref, aim_ref, ldt_ref, bre_ref, bim_ref,
                     abre_ref, abim_ref, bbre_ref, bbim_ref):
    a_re = are_ref[...]
    a_im = aim_ref[...]
    dt = jnp.exp(ldt_ref[...])
    mag = jnp.exp(dt * a_re)
    ab_re = mag * jnp.cos(dt * a_im)
    ab_im = mag * jnp.sin(dt * a_im)
    den = a_re * a_re + a_im * a_im
    nr = ab_re - 1.0
    co_re = (nr * a_re + ab_im * a_im) / den
    co_im = (ab_im * a_re - nr * a_im) / den
    abre_ref[...] = ab_re
    abim_ref[...] = ab_im
    b_re = bre_ref[...]
    b_im = bim_ref[...]
    bbre_ref[...] = co_re[:, None, :] * b_re - co_im[:, None, :] * b_im
    bbim_ref[...] = co_re[:, None, :] * b_im + co_im[:, None, :] * b_re


def _ssm_prep(a_re, a_im, log_dt, b_re, b_im):
    g, p = a_re.shape
    c = b_re.shape[-1]
    bt_re = jnp.transpose(b_re, (0, 2, 1))
    bt_im = jnp.transpose(b_im, (0, 2, 1))
    return pl.pallas_call(
        _ssm_prep_kernel,
        out_shape=(jax.ShapeDtypeStruct((g, p), F32), jax.ShapeDtypeStruct((g, p), F32),
                   jax.ShapeDtypeStruct((g, c, p), F32), jax.ShapeDtypeStruct((g, c, p), F32)),
        name="ssm_prep",
    )(a_re, a_im, log_dt.reshape(g, 1), bt_re, bt_im)


def _block_diag_tiles(w):
    g, r, cc = w.shape
    nt = g // GROUPS_PER_TILE
    w = w.reshape(nt, GROUPS_PER_TILE, r, cc)
    eye = jnp.eye(GROUPS_PER_TILE, dtype=w.dtype)
    return jnp.einsum('sgrc,gh->sgrhc', w, eye).reshape(nt, GROUPS_PER_TILE * r, GROUPS_PER_TILE * cc)


def _alibi_slopes(n_heads):
    return [2.0 ** (-8.0 * (h + 1) / n_heads) for h in range(n_heads)]


def _proj_kernel(x_ref, mod_ref, gpre_ref, wu_ref, wqt_ref, wk_ref, wvt_ref, wg_ref, perm_ref, kbias_ref,
                 u_ref, qt_ref, ka_ref, vt_ref, sg_ref, kmt_ref, *, n_heads, n_blk):
    i = pl.program_id(1)
    tm = x_ref.shape[0]
    width = n_heads * HEAD_DIM
    blk_per_tile = tm // MOBA_BLOCK

    @pl.when(i == 0)
    def _():
        kmt_ref[...] = jnp.zeros_like(kmt_ref)

    x = x_ref[...]
    ms = jnp.mean(x * x, axis=-1, keepdims=True)
    gain = gpre_ref[...] * (1.0 + mod_ref[1:2, :])
    hn = x * lax.rsqrt(ms + NORM_EPS) * gain + mod_ref[0:1, :]
    hb = hn.astype(BF16)

    q_t = _dot_nt(wqt_ref[...], hb)
    k = _dot(hb, wk_ref[...])

    head_of_lane = lax.broadcasted_iota(jnp.int32, (n_heads, width), 1) // HEAD_DIM
    head_mask = (head_of_lane == lax.broadcasted_iota(jnp.int32, (n_heads, width), 0)).astype(F32)
    for lb in range(blk_per_tile):
        km = jnp.mean(k[lb * MOBA_BLOCK:(lb + 1) * MOBA_BLOCK, :], axis=0, keepdims=True)
        row0 = pl.multiple_of((i * blk_per_tile + lb) * n_heads, n_heads)
        kmt_ref[pl.ds(row0, n_heads), :] = km * head_mask

    gate = _dot_split3(kmt_ref[...], q_t)
    u_ref[...] = _dot(hb, wu_ref[...]).astype(BF16)
    v_t = _dot_nt(wvt_ref[...], hb)
    gate = gate.reshape(n_blk, n_heads, tm)
    n_iota = lax.broadcasted_iota(jnp.int32, (n_blk, n_heads, tm), 0)
    own = i * blk_per_tile + lax.broadcasted_iota(jnp.int32, (n_blk, n_heads, tm), 2) // MOBA_BLOCK
    neg_inf = jnp.float32(-jnp.inf)
    cur = jnp.where(n_iota < own, gate, neg_inf)
    sel = jnp.zeros((n_blk, n_heads, tm), F32)
    for _ in range(MOBA_TOPK):
        best = jnp.max(cur, axis=0, keepdims=True)
        idx = jnp.min(jnp.where(cur == best, n_iota, n_blk), axis=0, keepdims=True)
        hit = (n_iota == idx) & (best > neg_inf)
        sel = jnp.where(hit, 1.0, sel)
        cur = jnp.where(hit, neg_inf, cur)
    unsel = 1.0 - sel - (n_iota == own).astype(F32)
    unsel = unsel.reshape(n_blk * n_heads, tm).astype(BF16)
    mask_rows = _dot(perm_ref[...], unsel) * NEG_BIG
    sg_ref[...] = jax.nn.sigmoid(_dot(hb, wg_ref[...])).astype(BF16)

    sub8 = lax.broadcasted_iota(jnp.int32, (SUBLANES, tm), 0)
    ones3 = (sub8 < 3).astype(F32)
    ones1 = (sub8 < 1).astype(F32)
    zeros = lambda r: jnp.zeros((r, tm), F32)
    lane = lax.broadcasted_iota(jnp.int32, (tm, LANES), 1)
    row = lax.broadcasted_iota(jnp.int32, (tm, LANES), 0)
    own_row = i * blk_per_tile + row // MOBA_BLOCK
    scale = HEAD_DIM ** -0.5 * LOG2E
    slopes = _alibi_slopes(n_heads)
    pad_feats = LANES - HEAD_DIM - MASK_FEATS - SUBLANES
    for h in range(n_heads):
        rows = slice(h * HEAD_DIM, (h + 1) * HEAD_DIM)
        tile = slice((h // 2) * LANES, (h // 2 + 1) * LANES)
        q_rows = q_t[rows] * scale
        m_rows = mask_rows[h * n_blk:(h + 1) * n_blk]
        if n_blk < MASK_FEATS:
            m_rows = jnp.concatenate([m_rows, zeros(MASK_FEATS - n_blk)], axis=0)
        if h % 2 == 0:
            qt = jnp.concatenate([q_rows, m_rows, ones3, zeros(pad_feats)], axis=0)
            data_lanes = lane < HEAD_DIM
            feat0 = HEAD_DIM
        else:
            qt = jnp.concatenate([m_rows, ones3, zeros(pad_feats), q_rows], axis=0)
            data_lanes = lane >= HEAD_DIM
            feat0 = 0
        qt_ref[h] = qt.astype(BF16)
        feats = jnp.where(lane == feat0 + own_row, 1.0, kbias_ref[h])
        ka = jnp.where(data_lanes, k[:, tile], feats).astype(BF16)
        vt = jnp.concatenate([v_t[rows], ones1, zeros(V_ROWS - HEAD_DIM - SUBLANES)], axis=0).astype(BF16)
        for kt in range(tm // KV_TILE):
            ka_ref[h, kt] = ka[kt * KV_TILE:(kt + 1) * KV_TILE]
            vt_ref[h, kt] = vt[:, kt * KV_TILE:(kt + 1) * KV_TILE]


def _leading_8_bits(x):
    c = x * jnp.float32(2 ** 16 + 1)
    return c - (c - x)


def _alibi_key_features(n_heads, tm):
    pos = (jnp.arange(tm) % MOBA_BLOCK).astype(F32)
    lane = jnp.arange(LANES)
    out = []
    for h, slope in enumerate(_alibi_slopes(n_heads)):
        feat0 = (HEAD_DIM if h % 2 == 0 else 0) + MASK_FEATS
        bias = (slope * LOG2E) * pos
        hi = _leading_8_bits(bias)
        mid = _leading_8_bits(bias - hi)
        lo = bias - hi - mid
        feats = jnp.zeros((tm, LANES), F32)
        for j, term in enumerate((hi, mid, lo)):
            feats = jnp.where(lane[None, :] == feat0 + j, term[:, None], feats)
        out.append(feats)
    return jnp.stack(out, axis=0)


def _row_permutation(n_blk, n_heads):
    r = jnp.arange(n_heads * n_blk)
    src = (r % n_blk) * n_heads + r // n_blk
    return (src[:, None] == jnp.arange(n_blk * n_heads)[None, :]).astype(BF16)


def _proj(x, mod, g_pre, w_in, n_heads, ssm_w, tm):
    bsz, s, d = x.shape
    n_t = s // tm
    half_t = n_t // 2
    n_blk = s // MOBA_BLOCK
    width = n_heads * HEAD_DIM
    assert n_blk <= MASK_FEATS and tm % KV_TILE == 0
    c = [0, ssm_w, ssm_w + width, ssm_w + 2 * width, ssm_w + 3 * width, w_in.shape[1]]
    gate_w = c[5] - c[4]
    wu = w_in[:, c[0]:c[1]].astype(BF16)
    wqt = w_in[:, c[1]:c[2]].T.astype(BF16)
    wk = w_in[:, c[2]:c[3]].astype(BF16)
    wvt = w_in[:, c[3]:c[4]].T.astype(BF16)
    wg = w_in[:, c[4]:c[5]].astype(BF16)
    perm = _row_permutation(n_blk, n_heads)
    kbias = _alibi_key_features(n_heads, tm)
    full = lambda a: pl.BlockSpec(a.shape, lambda b, i: (0,) * a.ndim, pipeline_mode=pl.Buffered(1))
    kern = functools.partial(_proj_kernel, n_heads=n_heads, n_blk=n_blk)
    return pl.pallas_call(
        kern,
        out_shape=(jax.ShapeDtypeStruct((s // 2, 2 * bsz * ssm_w), BF16),
                   jax.ShapeDtypeStruct((bsz, n_heads, LANES, s), BF16),
                   jax.ShapeDtypeStruct((bsz, n_heads, s // KV_TILE, KV_TILE, LANES), BF16),
                   jax.ShapeDtypeStruct((bsz, n_heads, s // KV_TILE, V_ROWS, KV_TILE), BF16),
                   jax.ShapeDtypeStruct((bsz, s, gate_w), BF16)),
        grid=(bsz, n_t),
        in_specs=[pl.BlockSpec((None, tm, d), lambda b, i: (b, i, 0)),
                  pl.BlockSpec((None, N_MOD, d), lambda b, i: (b, 0, 0)),
                  full(g_pre), full(wu), full(wqt), full(wk), full(wvt), full(wg), full(perm), full(kbias)],
        out_specs=(pl.BlockSpec((tm, ssm_w), lambda b, i: (i % half_t, (i // half_t) * bsz + b)),
                   pl.BlockSpec((None, n_heads, LANES, tm), lambda b, i: (b, 0, 0, i)),
                   pl.BlockSpec((None, n_heads, tm // KV_TILE, KV_TILE, LANES), lambda b, i: (b, 0, i, 0, 0)),
                   pl.BlockSpec((None, n_heads, tm // KV_TILE, V_ROWS, KV_TILE), lambda b, i: (b, 0, i, 0, 0)),
                   pl.BlockSpec((None, tm, gate_w), lambda b, i: (b, i, 0))),
        scratch_shapes=[pltpu.VMEM((n_blk * n_heads, width), F32)],
        compiler_params=_cparams(("arbitrary", "arbitrary")),
        name="proj",
    )(x, mod, g_pre, wu, wqt, wk, wvt, wg, perm, kbias)


def _gelu_tanh(x):
    c = math.sqrt(2.0 / math.pi)
    return 0.5 * x * (1.0 + jnp.tanh(c * (x + 0.044715 * (x * x * x))))


def _ssm_kernel(u_ref, un_ref, perm_ref, permt_ref, wb_ref, a_ref, wc_ref, d_ref, init_ref, *rest, emit):
    if emit:
        z_ref, fin_ref, bua_ref, bub_ref, usa_ref, usb_ref, st_ref, zs_ref = rest
    else:
        fin_ref, bua_ref, bub_ref, usa_ref, usb_ref, st_ref = rest
    i = pl.program_id(0)
    steps = un_ref.shape[0]
    ssm_w = u_ref.shape[1] // SUBLANES
    n_tiles = wb_ref.shape[0]
    st_w = wb_ref.shape[2] // 2

    def project(src_ref, t0, bu_ref, us_ref):
        u_nat = jnp.concatenate([src_ref[t0:t0 + steps, s * ssm_w:(s + 1) * ssm_w]
                                 for s in range(SUBLANES)], axis=0)
        us_ref[...] = _dot(perm_ref[...], u_nat).astype(BF16)
        for sg in range(n_tiles):
            r = _dot(us_ref[:, sg * LANES:(sg + 1) * LANES], wb_ref[sg])
            bu_ref[0, :, sg * st_w:(sg + 1) * st_w] = r[:, :st_w]
            bu_ref[1, :, sg * st_w:(sg + 1) * st_w] = r[:, st_w:]

    def scan(bu_ref, x_re, x_im):
        for t in range(steps):
            rows = slice(t * SUBLANES, (t + 1) * SUBLANES)
            a_re = a_ref[0]
            a_im = a_ref[1]
            n_re = a_re * x_re - a_im * x_im + bu_ref[0, rows, :]
            n_im = a_re * x_im + a_im * x_re + bu_ref[1, rows, :]
            bu_ref[0, rows, :] = n_re
            bu_ref[1, rows, :] = n_im
            x_re, x_im = n_re, n_im
        return x_re, x_im

    def output(bu_ref, us_ref, t0):
        for sg in range(n_tiles):
            xr = bu_ref[0, :, sg * st_w:(sg + 1) * st_w].astype(BF16)
            xi = bu_ref[1, :, sg * st_w:(sg + 1) * st_w].astype(BF16)
            y = _dot(xr, wc_ref[sg, :st_w, :]) + _dot(xi, wc_ref[sg, st_w:, :])
            lanes = slice(sg * LANES, (sg + 1) * LANES)
            y = y + d_ref[:, lanes] * us_ref[:, lanes].astype(F32)
            zs_ref[:, lanes] = _gelu_tanh(y).astype(BF16)
        z_nat = _dot(permt_ref[...], zs_ref[...]).astype(BF16)
        for s in range(SUBLANES):
            z_ref[t0:t0 + steps, s * ssm_w:(s + 1) * ssm_w] = z_nat[s * steps:(s + 1) * steps]

    @pl.when(i == 0)
    def _():
        st_ref[...] = init_ref[...]
        project(u_ref, 0, bua_ref, usa_ref)

    project(u_ref, steps, bub_ref, usb_ref)
    x_re, x_im = scan(bua_ref, st_ref[0], st_ref[1])
    if emit:
        output(bua_ref, usa_ref, 0)
    project(un_ref, 0, bua_ref, usa_ref)
    x_re, x_im = scan(bub_ref, x_re, x_im)
    if emit:
        output(bub_ref, usb_ref, steps)
    st_ref[0] = x_re
    st_ref[1] = x_im
    fin_ref[0] = x_re
    fin_ref[1] = x_im


def _ssm_pass(u_tm, wb, a_cat, wc, d_row, init, steps_per_tile, emit):
    n_steps, slot_w = u_tm.shape
    ssm_w = slot_w // SUBLANES
    rows = steps_per_tile * SUBLANES
    n_tiles = n_steps // steps_per_tile
    assert n_tiles % 2 == 0
    st_lanes = a_cat.shape[2]
    r = jnp.arange(rows)
    src = (r % SUBLANES) * steps_per_tile + r // SUBLANES
    perm = (src[:, None] == r[None, :]).astype(BF16)
    perm_t = perm.T
    fin_shape = jax.ShapeDtypeStruct((2, SUBLANES, st_lanes), F32)
    fin_spec = pl.BlockSpec((2, SUBLANES, st_lanes), lambda i: (0, 0, 0))
    scratch = [pltpu.VMEM((2, rows, st_lanes), F32), pltpu.VMEM((2, rows, st_lanes), F32),
               pltpu.VMEM((rows, ssm_w), BF16), pltpu.VMEM((rows, ssm_w), BF16),
               pltpu.VMEM((2, SUBLANES, st_lanes), F32)]
    if emit:
        out_shape = (jax.ShapeDtypeStruct((n_steps, slot_w), BF16), fin_shape)
        out_specs = (pl.BlockSpec((2 * steps_per_tile, slot_w), lambda i: (i, 0)), fin_spec)
        scratch.append(pltpu.VMEM((rows, ssm_w), BF16))
    else:
        out_shape = fin_shape
        out_specs = fin_spec
    return pl.pallas_call(
        functools.partial(_ssm_kernel, emit=emit),
        out_shape=out_shape,
        grid=(n_tiles // 2,),
        in_specs=[pl.BlockSpec((2 * steps_per_tile, slot_w), lambda i: (i, 0)),
                  pl.BlockSpec((steps_per_tile, slot_w), lambda i: (jnp.minimum(2 * i + 2, n_tiles - 1), 0)),
                  pl.BlockSpec(perm.shape, lambda i: (0, 0)),
                  pl.BlockSpec(perm.shape, lambda i: (0, 0)),
                  pl.BlockSpec(wb.shape, lambda i: (0, 0, 0)),
                  pl.BlockSpec(a_cat.shape, lambda i: (0, 0, 0)),
                  pl.BlockSpec(wc.shape, lambda i: (0, 0, 0)),
                  pl.BlockSpec(d_row.shape, lambda i: (0, 0)),
                  pl.BlockSpec(init.shape, lambda i: (0, 0, 0))],
        out_specs=out_specs,
        scratch_shapes=scratch,
        compiler_params=_cparams(("arbitrary",)),
        name="ssm_scan" if emit else "ssm_state",
    )(u_tm, u_tm, perm, perm_t, wb, a_cat, wc, d_row, init)


def _attn_kernel(slope_ref, qt_ref, ka_ref, vt_ref, o_ref, acc_ref, sa_ref, sb_ref):
    hp = pl.program_id(1)
    i = pl.program_id(2)
    tq = qt_ref.shape[2]
    key_row = lax.broadcasted_iota(jnp.int32, (KV_TILE, 1), 0)
    q_lane = lax.broadcasted_iota(jnp.int32, (1, tq), 1)
    causal = key_row <= q_lane
    acc_ref[...] = jnp.zeros_like(acc_ref)
    halves = KV_TILE // MOBA_BLOCK

    def scores(kt, s_ref):
        for hh in range(2):
            s_ref[hh] = _dot(ka_ref[hh, kt], qt_ref[hh])

    def col_max(s_t):
        rows = s_t.shape[0]
        while rows > SUBLANES:
            rows //= 2
            s_t = jnp.maximum(s_t[:rows], s_t[rows:])
        return jnp.max(s_t, axis=0, keepdims=True)

    def step(kt, s_ref, ms, last):
        new_ms = []
        for hh in range(2):
            slope = slope_ref[hp * 2 + hh]
            parts, shifts = [], []
            for d in range(halves):
                rows = slice(d * MOBA_BLOCK, (d + 1) * MOBA_BLOCK)
                s_t = s_ref[hh, rows, :]
                if last:
                    s_t = jnp.where(causal[rows], s_t, NEG_BIG)
                parts.append(s_t)
                shifts.append(slope * (((kt - i) * halves + d) * MOBA_BLOCK).astype(F32))
            m_new = ms[hh]
            for s_t, shift in zip(parts, shifts):
                m_new = jnp.maximum(m_new, col_max(s_t) + shift)
            p = jnp.concatenate([jnp.exp2(s_t - (m_new - shift)) for s_t, shift in zip(parts, shifts)],
                                axis=0).astype(BF16)
            acc_ref[hh] = acc_ref[hh] * jnp.exp2(ms[hh] - m_new) + _dot(vt_ref[hh, kt], p)
            new_ms.append(m_new)
        return tuple(new_ms)

    m0 = jnp.full((1, tq), 0.1 * NEG_BIG, F32)
    last_kt = i

    def run_tiles(first, count, ms, ends_sequence):
        bufs = (sa_ref, sb_ref)
        for j in range(count):
            if j + 1 < count or not ends_sequence:
                scores(first + j + 1, bufs[(j + 1) % 2])
            ms = step(first + j, bufs[j % 2], ms, ends_sequence and j + 1 == count)
        return ms

    scores(0, sa_ref)
    ms = lax.fori_loop(0, last_kt // TILES_PER_ITER,
                       lambda q, ms: run_tiles(q * TILES_PER_ITER, TILES_PER_ITER, ms, False), (m0, m0))
    rem = last_kt % TILES_PER_ITER
    for k in range(TILES_PER_ITER):
        @pl.when(rem == k)
        def _(k=k):
            run_tiles(last_kt - k, k + 1, ms, True)


    outs = []
    for hh in range(2):
        acc = acc_ref[hh]
        outs.append(acc[:HEAD_DIM] / acc[HEAD_DIM:HEAD_DIM + 1])
    o_ref[...] = jnp.transpose(jnp.concatenate(outs, axis=0)).astype(o_ref.dtype)


def _attention(qt, ka, vt):
    bsz, n_heads, _, s = qt.shape
    n_kt = ka.shape[2]
    tq = KV_TILE
    slopes = jnp.asarray([sl * LOG2E for sl in _alibi_slopes(n_heads)], F32)
    return pl.pallas_call(
        _attn_kernel,
        out_shape=jax.ShapeDtypeStruct((bsz, s, n_heads * HEAD_DIM), BF16),
        grid=(bsz, n_heads // 2, n_kt),
        in_specs=[pl.BlockSpec(memory_space=pltpu.SMEM),
                  pl.BlockSpec((None, 2, LANES, tq), lambda b, hp, i: (b, hp, 0, i)),
                  pl.BlockSpec((None, 2, n_kt, KV_TILE, LANES), lambda b, hp, i: (b, hp, 0, 0, 0)),
                  pl.BlockSpec((None, 2, n_kt, V_ROWS, KV_TILE), lambda b, hp, i: (b, hp, 0, 0, 0))],
        out_specs=pl.BlockSpec((None, tq, LANES), lambda b, hp, i: (b, i, hp)),
        scratch_shapes=[pltpu.VMEM((2, V_ROWS, tq), F32),
                        pltpu.VMEM((2, KV_TILE, tq), F32),
                        pltpu.VMEM((2, KV_TILE, tq), F32)],
        compiler_params=_cparams(("arbitrary", "arbitrary", "arbitrary")),
        name="attn",
    )(slopes, qt, ka, vt)


def _rms(y, g):
    return y * lax.rsqrt(jnp.mean(y * y, axis=-1, keepdims=True) + NORM_EPS) * g


def _merge_kernel(x_ref, z_ref, o_ref, sg_ref, mod_ref, wa_ref, wb_ref, wo_ref, wout_ref, gpost_ref, out_ref):
    d = x_ref.shape[1]
    z = z_ref[...]
    y_a = _dot(z, wa_ref[...]) * jax.nn.sigmoid(_dot(z, wb_ref[...]))
    y_b = _dot(o_ref[...], wo_ref[...])
    merged = sg_ref[:, :d].astype(F32) * y_a + sg_ref[:, d:].astype(F32) * y_b
    y = _dot(merged.astype(BF16), wout_ref[...])
    out_ref[...] = x_ref[...] + mod_ref[2:3, :] * _rms(y, gpost_ref[...])


def _merge(x, z_tm, o, sg, mod, wa, wb, wo, wout, g_post, tm):
    bsz, s, d = x.shape
    n_t = s // tm
    half_t = n_t // 2
    ssm_w = wa.shape[0]
    full = lambda a: pl.BlockSpec(a.shape, lambda b, i: (0,) * a.ndim, pipeline_mode=pl.Buffered(1))
    return pl.pallas_call(
        _merge_kernel,
        out_shape=jax.ShapeDtypeStruct((bsz, s, d), F32),
        grid=(bsz, n_t),
        in_specs=[pl.BlockSpec((None, tm, d), lambda b, i: (b, i, 0)),
                  pl.BlockSpec((tm, ssm_w), lambda b, i: (i % half_t, (i // half_t) * bsz + b)),
                  pl.BlockSpec((None, tm, o.shape[2]), lambda b, i: (b, i, 0)),
                  pl.BlockSpec((None, tm, sg.shape[2]), lambda b, i: (b, i, 0)),
                  pl.BlockSpec((None, N_MOD, d), lambda b, i: (b, 0, 0)),
                  full(wa), full(wb), full(wo), full(wout), full(g_post)],
        out_specs=pl.BlockSpec((None, tm, d), lambda b, i: (b, i, 0)),
        compiler_params=_cparams(("arbitrary", "arbitrary")),
        name="merge",
    )(x, z_tm, o, sg, mod, wa, wb, wo, wout, g_post)


def _ffn_kernel(x_ref, mod_ref, gpre_ref, wg_ref, wu_ref, wd_ref, gpost_ref, out_ref, *, n_chunks):
    x = x_ref[...]
    hn = _rms(x, gpre_ref[...]) * (1.0 + mod_ref[4:5, :]) + mod_ref[3:4, :]
    hb = hn.astype(BF16)
    ff = wg_ref.shape[1]
    cw = ff // n_chunks
    y = None
    for c in range(n_chunks):
        cols = slice(c * cw, (c + 1) * cw)
        gate = _dot(hb, wg_ref[:, cols])
        up = _dot(hb, wu_ref[:, cols])
        mid = (gate * jax.nn.sigmoid(gate) * up).astype(BF16)
        part = _dot(mid, wd_ref[cols, :])
        y = part if y is None else y + part
    out_ref[...] = x + mod_ref[5:6, :] * _rms(y, gpost_ref[...])


def _ffn(x, mod, g_pre, wg, wu, wd, g_post, tm):
    bsz, s, d = x.shape
    ff = wg.shape[1]
    n_chunks = 2 if (ff // 2) % LANES == 0 else 1
    full = lambda a: pl.BlockSpec(a.shape, lambda b, i: (0,) * a.ndim, pipeline_mode=pl.Buffered(1))
    return pl.pallas_call(
        functools.partial(_ffn_kernel, n_chunks=n_chunks),
        out_shape=jax.ShapeDtypeStruct((bsz, s, d), F32),
        grid=(bsz, s // tm),
        in_specs=[pl.BlockSpec((None, tm, d), lambda b, i: (b, i, 0)),
                  pl.BlockSpec((None, N_MOD, d), lambda b, i: (b, 0, 0)),
                  full(g_pre), full(wg), full(wu), full(wd), full(g_post)],
        out_specs=pl.BlockSpec((None, tm, d), lambda b, i: (b, i, 0)),
        compiler_params=_cparams(("arbitrary", "arbitrary")),
        name="ffn",
    )(x, mod, g_pre, wg, wu, wd, g_post)


def _layer(x, mod, g_pre_mix, g_post_mix, w_in, ssm_a_re, ssm_a_im, ssm_log_dt, ssm_b_re, ssm_b_im,
           ssm_c_re, ssm_c_im, ssm_d, w_glu_a, w_glu_b, w_attn_out, w_out,
           g_pre_ffn, g_post_ffn, w_ff_gate, w_ff_up, w_ff_down):
    bsz, s, d = x.shape
    n_groups, n_state = ssm_a_re.shape
    ssm_w = n_groups * SSM_GROUP
    width = w_attn_out.shape[0]
    n_heads = width // HEAD_DIM
    assert 2 * bsz == SUBLANES, "SSM scan packs 2 sequence halves x batch into the 8 sublanes"
    tm = min(1024, s // 2)
    steps_per_tile = min(64, s // 2)
    assert s % (2 * tm) == 0 and tm % MOBA_BLOCK == 0 and (s // 2) % steps_per_tile == 0

    mod3 = mod.reshape(bsz, N_MOD, d)
    row = lambda g: g.reshape(1, -1)

    ab_re, ab_im, bbt_re, bbt_im = _ssm_prep(ssm_a_re, ssm_a_im, ssm_log_dt, ssm_b_re, ssm_b_im)
    wb = jnp.concatenate([_block_diag_tiles(bbt_re), _block_diag_tiles(bbt_im)], axis=-1).astype(BF16)
    wc = jnp.concatenate([_block_diag_tiles(jnp.transpose(ssm_c_re, (0, 2, 1))),
                          _block_diag_tiles(-jnp.transpose(ssm_c_im, (0, 2, 1)))], axis=1).astype(BF16)
    a_cat = jnp.stack([ab_re.reshape(-1), ab_im.reshape(-1)], axis=0)
    a_cat = jnp.broadcast_to(a_cat[:, None, :], (2, SUBLANES, a_cat.shape[1]))
    d_row = ssm_d.reshape(1, ssm_w)

    u_tm, qt, ka, vt, sg = _proj(x, mod3, row(g_pre_mix), w_in, n_heads, ssm_w, tm)

    zero_state = jnp.zeros((2, SUBLANES, n_groups * n_state), F32)
    first = _ssm_pass(u_tm, wb, a_cat, wc, d_row, zero_state, steps_per_tile, emit=False)
    init = jnp.concatenate([jnp.zeros_like(first[:, :bsz]), first[:, :bsz]], axis=1)
    z_tm, _ = _ssm_pass(u_tm, wb, a_cat, wc, d_row, init, steps_per_tile, emit=True)

    o = _attention(qt, ka, vt)

    x1 = _merge(x, z_tm, o, sg, mod3, w_glu_a.astype(BF16), w_glu_b.astype(BF16),
                w_attn_out.astype(BF16), w_out.astype(BF16), row(g_post_mix), tm)
    return _ffn(x1, mod3, row(g_pre_ffn), w_ff_gate.astype(BF16), w_ff_up.astype(BF16),
                w_ff_down.astype(BF16), row(g_post_ffn), tm)


def kernel(x, c, w_ada, b_ada, g_pre_mix, g_post_mix, w_in, ssm_a_re, ssm_a_im, ssm_log_dt, ssm_b_re, ssm_b_im, ssm_c_re, ssm_c_im, ssm_d, w_glu_a, w_glu_b, w_attn_out, w_out, g_pre_ffn, g_post_ffn, w_ff_gate, w_ff_up, w_ff_down):
    depth = w_ada.shape[0]
    for l in range(depth):
        mod = _adaln(c, w_ada[l], b_ada[l])
        x = _layer(x, mod, g_pre_mix[l], g_post_mix[l], w_in[l], ssm_a_re[l], ssm_a_im[l], ssm_log_dt[l],
                   ssm_b_re[l], ssm_b_im[l], ssm_c_re[l], ssm_c_im[l], ssm_d[l], w_glu_a[l], w_glu_b[l],
                   w_attn_out[l], w_out[l], g_pre_ffn[l], g_post_ffn[l], w_ff_gate[l], w_ff_up[l], w_ff_down[l])
    return x
```

```python
import functools
import math

import jax
import jax.numpy as jnp
from jax import lax
from jax.experimental import pallas as pl
from jax.experimental.pallas import tpu as pltpu

F32 = jnp.float32
BF16 = jnp.bfloat16

SSM_GROUP = 16
SSM_STATE = 64
HEAD_DIM = 64
MOBA_BLOCK = 256
MOBA_TOPK = 3
N_MOD = 6
NORM_EPS = 1e-6
LANES = 128
SUBLANES = 8
GROUPS_PER_TILE = LANES // SSM_GROUP
NEG_BIG = -1e30
LOG2E = math.log2(math.e)
MASK_FEATS = 32
V_ROWS = HEAD_DIM + 16
KV_TILE = 2 * MOBA_BLOCK
SSM_TILES_PER_STEP = 4
TILES_PER_ITER = 8
VMEM_LIMIT = 56 * 1024 * 1024


def _cparams(sem):
    return pltpu.CompilerParams(dimension_semantics=sem, vmem_limit_bytes=VMEM_LIMIT)


def _dot(a, b):
    return jnp.dot(a, b, preferred_element_type=F32)


def _dot_split3(a, b):
    a_hi = a.astype(BF16)
    b_hi = b.astype(BF16)
    a_lo = (a - a_hi.astype(F32)).astype(BF16)
    b_lo = (b - b_hi.astype(F32)).astype(BF16)
    return _dot(a_hi, b_hi) + (_dot(a_hi, b_lo) + _dot(a_lo, b_hi))


def _dot_nt(a, b):
    return lax.dot_general(a, b, (((1,), (1,)), ((), ())), preferred_element_type=F32)


def _adaln_kernel(c_ref, w_ref, b_ref, o_ref):
    c = c_ref[...]
    act = c * jax.nn.sigmoid(c)
    o_ref[...] = jnp.dot(act, w_ref[...], preferred_element_type=F32,
                         precision=lax.Precision.HIGHEST) + b_ref[...]


def _adaln(c, w_ada, b_ada):
    bsz, d = c.shape
    n = w_ada.shape[1]
    tn = n // N_MOD
    return pl.pallas_call(
        _adaln_kernel,
        out_shape=jax.ShapeDtypeStruct((bsz, n), F32),
        grid=(n // tn,),
        in_specs=[pl.BlockSpec((bsz, d), lambda j: (0, 0)),
                  pl.BlockSpec((d, tn), lambda j: (0, j)),
                  pl.BlockSpec((1, tn), lambda j: (0, j))],
        out_specs=pl.BlockSpec((bsz, tn), lambda j: (0, j)),
        compiler_params=_cparams(("arbitrary",)),
        name="adaln",
    )(c, w_ada, b_ada.reshape(1, n))


def _ssm_prep_kernel(are_ref, aim_ref, ldt_ref, bre_ref, bim_ref,
                     abre_ref, abim_ref, bbre_ref, bbim_ref):
    a_re = are_ref[...]
    a_im = aim_ref[...]
    dt = jnp.exp(ldt_ref[...])
    mag = jnp.exp(dt * a_re)
    ab_re = mag * jnp.cos(dt * a_im)
    ab_im = mag * jnp.sin(dt * a_im)
    den = a_re * a_re + a_im * a_im
    nr = ab_re - 1.0
    co_re = (nr * a_re + ab_im * a_im) / den
    co_im = (ab_im * a_re - nr * a_im) / den
    abre_ref[...] = ab_re
    abim_ref[...] = ab_im
    b_re = bre_ref[...]
    b_im = bim_ref[...]
    bbre_ref[...] = co_re[:, None, :] * b_re - co_im[:, None, :] * b_im
    bbim_ref[...] = co_re[:, None, :] * b_im + co_im[:, None, :] * b_re


def _ssm_prep(a_re, a_im, log_dt, b_re, b_im):
    g, p = a_re.shape
    c = b_re.shape[-1]
    bt_re = jnp.transpose(b_re, (0, 2, 1))
    bt_im = jnp.transpose(b_im, (0, 2, 1))
    return pl.pallas_call(
        _ssm_prep_kernel,
        out_shape=(jax.ShapeDtypeStruct((g, p), F32), jax.ShapeDtypeStruct((g, p), F32),
                   jax.ShapeDtypeStruct((g, c, p), F32), jax.ShapeDtypeStruct((g, c, p), F32)),
        name="ssm_prep",
    )(a_re, a_im, log_dt.reshape(g, 1), bt_re, bt_im)


def _block_diag_tiles(w):
    g, r, cc = w.shape
    nt = g // GROUPS_PER_TILE
    w = w.reshape(nt, GROUPS_PER_TILE, r, cc)
    eye = jnp.eye(GROUPS_PER_TILE, dtype=w.dtype)
    return jnp.einsum('sgrc,gh->sgrhc', w, eye).reshape(nt, GROUPS_PER_TILE * r, GROUPS_PER_TILE * cc)


def _alibi_slopes(n_heads):
    return [2.0 ** (-8.0 * (h + 1) / n_heads) for h in range(n_heads)]


def _proj_kernel(x_ref, mod_ref, gpre_ref, wu_ref, wqt_ref, wk_ref, wvt_ref, wg_ref, perm_ref, kbias_ref,
                 u_ref, qt_ref, ka_ref, vt_ref, sg_ref, kmt_ref, *, n_heads, n_blk):
    i = pl.program_id(1)
    tm = x_ref.shape[0]
    width = n_heads * HEAD_DIM
    blk_per_tile = tm // MOBA_BLOCK

    @pl.when(i == 0)
    def _():
        kmt_ref[...] = jnp.zeros_like(kmt_ref)

    x = x_ref[...]
    ms = jnp.mean(x * x, axis=-1, keepdims=True)
    gain = gpre_ref[...] * (1.0 + mod_ref[1:2, :])
    hn = x * lax.rsqrt(ms + NORM_EPS) * gain + mod_ref[0:1, :]
    hb = hn.astype(BF16)

    q_t = _dot_nt(wqt_ref[...], hb)
    k = _dot(hb, wk_ref[...])

    head_of_lane = lax.broadcasted_iota(jnp.int32, (n_heads, width), 1) // HEAD_DIM
    head_mask = (head_of_lane == lax.broadcasted_iota(jnp.int32, (n_heads, width), 0)).astype(F32)
    for lb in range(blk_per_tile):
        km = jnp.mean(k[lb * MOBA_BLOCK:(lb + 1) * MOBA_BLOCK, :], axis=0, keepdims=True)
        row0 = pl.multiple_of((i * blk_per_tile + lb) * n_heads, n_heads)
        kmt_ref[pl.ds(row0, n_heads), :] = km * head_mask

    gate = _dot_split3(kmt_ref[...], q_t)
    u_ref[...] = _dot(hb, wu_ref[...]).astype(BF16)
    v_t = _dot_nt(wvt_ref[...], hb)
    gate = gate.reshape(n_blk, n_heads, tm)
    n_iota = lax.broadcasted_iota(jnp.int32, (n_blk, n_heads, tm), 0)
    own = i * blk_per_tile + lax.broadcasted_iota(jnp.int32, (n_blk, n_heads, tm), 2) // MOBA_BLOCK
    neg_inf = jnp.float32(-jnp.inf)
    cur = jnp.where(n_iota < own, gate, neg_inf)
    sel = jnp.zeros((n_blk, n_heads, tm), F32)
    for _ in range(MOBA_TOPK):
        best = jnp.max(cur, axis=0, keepdims=True)
        idx = jnp.min(jnp.where(cur == best, n_iota, n_blk), axis=0, keepdims=True)
        hit = (n_iota == idx) & (best > neg_inf)
        sel = jnp.where(hit, 1.0, sel)
        cur = jnp.where(hit, neg_inf, cur)
    unsel = 1.0 - sel - (n_iota == own).astype(F32)
    unsel = unsel.reshape(n_blk * n_heads, tm).astype(BF16)
    mask_rows = _dot(perm_ref[...], unsel) * NEG_BIG
    sg_ref[...] = jax.nn.sigmoid(_dot(hb, wg_ref[...])).astype(BF16)

    sub8 = lax.broadcasted_iota(jnp.int32, (SUBLANES, tm), 0)
    ones3 = (sub8 < 3).astype(F32)
    ones1 = (sub8 < 1).astype(F32)
    zeros = lambda r: jnp.zeros((r, tm), F32)
    lane = lax.broadcasted_iota(jnp.int32, (tm, LANES), 1)
    row = lax.broadcasted_iota(jnp.int32, (tm, LANES), 0)
    own_row = i * blk_per_tile + row // MOBA_BLOCK
    scale = HEAD_DIM ** -0.5 * LOG2E
    slopes = _alibi_slopes(n_heads)
    pad_feats = LANES - HEAD_DIM - MASK_FEATS - SUBLANES
    for h in range(n_heads):
        rows = slice(h * HEAD_DIM, (h + 1) * HEAD_DIM)
        tile = slice((h // 2) * LANES, (h // 2 + 1) * LANES)
        q_rows = q_t[rows] * scale
        m_rows = mask_rows[h * n_blk:(h + 1) * n_blk]
        if n_blk < MASK_FEATS:
            m_rows = jnp.concatenate([m_rows, zeros(MASK_FEATS - n_blk)], axis=0)
        if h % 2 == 0:
            qt = jnp.concatenate([q_rows, m_rows, ones3, zeros(pad_feats)], axis=0)
            data_lanes = lane < HEAD_DIM
            feat0 = HEAD_DIM
        else:
            qt = jnp.concatenate([m_rows, ones3, zeros(pad_feats), q_rows], axis=0)
            data_lanes = lane >= HEAD_DIM
            feat0 = 0
        qt_ref[h] = qt.astype(BF16)
        feats = jnp.where(lane == feat0 + own_row, 1.0, kbias_ref[h])
        ka = jnp.where(data_lanes, k[:, tile], feats).astype(BF16)
        vt = jnp.concatenate([v_t[rows], ones1, zeros(V_ROWS - HEAD_DIM - SUBLANES)], axis=0).astype(BF16)
        for kt in range(tm // KV_TILE):
            ka_ref[h, kt] = ka[kt * KV_TILE:(kt + 1) * KV_TILE]
            vt_ref[h, kt] = vt[:, kt * KV_TILE:(kt + 1) * KV_TILE]


def _leading_8_bits(x):
    c = x * jnp.float32(2 ** 16 + 1)
    return c - (c - x)


def _alibi_key_features(n_heads, tm):
    pos = (jnp.arange(tm) % MOBA_BLOCK).astype(F32)
    lane = jnp.arange(LANES)
    out = []
    for h, slope in enumerate(_alibi_slopes(n_heads)):
        feat0 = (HEAD_DIM if h % 2 == 0 else 0) + MASK_FEATS
        bias = (slope * LOG2E) * pos
        hi = _leading_8_bits(bias)
        mid = _leading_8_bits(bias - hi)
        lo = bias - hi - mid
        feats = jnp.zeros((tm, LANES), F32)
        for j, term in enumerate((hi, mid, lo)):
            feats = jnp.where(lane[None, :] == feat0 + j, term[:, None], feats)
        out.append(feats)
    return jnp.stack(out, axis=0)


def _row_permutation(n_blk, n_heads):
    r = jnp.arange(n_heads * n_blk)
    src = (r % n_blk) * n_heads + r // n_blk
    return (src[:, None] == jnp.arange(n_blk * n_heads)[None, :]).astype(BF16)


def _proj(x, mod, g_pre, w_in, n_heads, ssm_w, tm):
    bsz, s, d = x.shape
    n_t = s // tm
    half_t = n_t // 2
    n_blk = s // MOBA_BLOCK
    width = n_heads * HEAD_DIM
    assert n_blk <= MASK_FEATS and tm % KV_TILE == 0
    c = [0, ssm_w, ssm_w + width, ssm_w + 2 * width, ssm_w + 3 * width, w_in.shape[1]]
    gate_w = c[5] - c[4]
    wu = w_in[:, c[0]:c[1]].astype(BF16)
    wqt = w_in[:, c[1]:c[2]].T.astype(BF16)
    wk = w_in[:, c[2]:c[3]].astype(BF16)
    wvt = w_in[:, c[3]:c[4]].T.astype(BF16)
    wg = w_in[:, c[4]:c[5]].astype(BF16)
    perm = _row_permutation(n_blk, n_heads)
    kbias = _alibi_key_features(n_heads, tm)
    full = lambda a: pl.BlockSpec(a.shape, lambda b, i: (0,) * a.ndim, pipeline_mode=pl.Buffered(1))
    kern = functools.partial(_proj_kernel, n_heads=n_heads, n_blk=n_blk)
    return pl.pallas_call(
        kern,
        out_shape=(jax.ShapeDtypeStruct((s // 2, 2 * bsz * ssm_w), BF16),
                   jax.ShapeDtypeStruct((bsz, n_heads, LANES, s), BF16),
                   jax.ShapeDtypeStruct((bsz, n_heads, s // KV_TILE, KV_TILE, LANES), BF16),
                   jax.ShapeDtypeStruct((bsz, n_heads, s // KV_TILE, V_ROWS, KV_TILE), BF16),
                   jax.ShapeDtypeStruct((bsz, s, gate_w), BF16)),
        grid=(bsz, n_t),
        in_specs=[pl.BlockSpec((None, tm, d), lambda b, i: (b, i, 0)),
                  pl.BlockSpec((None, N_MOD, d), lambda b, i: (b, 0, 0)),
                  full(g_pre), full(wu), full(wqt), full(wk), full(wvt), full(wg), full(perm), full(kbias)],
        out_specs=(pl.BlockSpec((tm, ssm_w), lambda b, i: (i % half_t, (i // half_t) * bsz + b)),
                   pl.BlockSpec((None, n_heads, LANES, tm), lambda b, i: (b, 0, 0, i)),
                   pl.BlockSpec((None, n_heads, tm // KV_TILE, KV_TILE, LANES), lambda b, i: (b, 0, i, 0, 0)),
                   pl.BlockSpec((None, n_heads, tm // KV_TILE, V_ROWS, KV_TILE), lambda b, i: (b, 0, i, 0, 0)),
                   pl.BlockSpec((None, tm, gate_w), lambda b, i: (b, i, 0))),
        scratch_shapes=[pltpu.VMEM((n_blk * n_heads, width), F32)],
        compiler_params=_cparams(("arbitrary", "arbitrary")),
        name="proj",
    )(x, mod, g_pre, wu, wqt, wk, wvt, wg, perm, kbias)


def _gelu_tanh(x):
    c = math.sqrt(2.0 / math.pi)
    return 0.5 * x * (1.0 + jnp.tanh(c * (x + 0.044715 * (x * x * x))))


def _ssm_kernel(u_ref, un_ref, perm_ref, permt_ref, wb_ref, a_ref, wc_ref, d_ref, init_ref, *rest, emit):
    if emit:
        z_ref, fin_ref, bua_ref, bub_ref, usa_ref, usb_ref, st_ref, zs_ref = rest
    else:
        fin_ref, bua_ref, bub_ref, usa_ref, usb_ref, st_ref = rest
    i = pl.program_id(0)
    steps = un_ref.shape[0]
    ssm_w = u_ref.shape[1] // SUBLANES
    n_tiles = wb_ref.shape[0]
    st_w = wb_ref.shape[2] // 2

    def project(src_ref, t0, bu_ref, us_ref):
        u_nat = jnp.concatenate([src_ref[t0:t0 + steps, s * ssm_w:(s + 1) * ssm_w]
                                 for s in range(SUBLANES)], axis=0)
        us_ref[...] = _dot(perm_ref[...], u_nat).astype(BF16)
        for sg in range(n_tiles):
            r = _dot(us_ref[:, sg * LANES:(sg + 1) * LANES], wb_ref[sg])
            bu_ref[0, :, sg * st_w:(sg + 1) * st_w] = r[:, :st_w]
            bu_ref[1, :, sg * st_w:(sg + 1) * st_w] = r[:, st_w:]

    def scan(bu_ref, x_re, x_im):
        for t in range(steps):
            rows = slice(t * SUBLANES, (t + 1) * SUBLANES)
            a_re = a_ref[0]
            a_im = a_ref[1]
            n_re = a_re * x_re - a_im * x_im + bu_ref[0, rows, :]
            n_im = a_re * x_im + a_im * x_re + bu_ref[1, rows, :]
            bu_ref[0, rows, :] = n_re
            bu_ref[1, rows, :] = n_im
            x_re, x_im = n_re, n_im
        return x_re, x_im

    def output(bu_ref, us_ref, t0):
        for sg in range(n_tiles):
            xr = bu_ref[0, :, sg * st_w:(sg + 1) * st_w].astype(BF16)
            xi = bu_ref[1, :, sg * st_w:(sg + 1) * st_w].astype(BF16)
            y = _dot(xr, wc_ref[sg, :st_w, :]) + _dot(xi, wc_ref[sg, st_w:, :])
            lanes = slice(sg * LANES, (sg + 1) * LANES)
            y = y + d_ref[:, lanes] * us_ref[:, lanes].astype(F32)
            zs_ref[:, lanes] = _gelu_tanh(y).astype(BF16)
        z_nat = _dot(permt_ref[...], zs_ref[...]).astype(BF16)
        for s in range(SUBLANES):
            z_ref[t0:t0 + steps, s * ssm_w:(s + 1) * ssm_w] = z_nat[s * steps:(s + 1) * steps]

    @pl.when(i == 0)
    def _():
        st_ref[...] = init_ref[...]
        project(u_ref, 0, bua_ref, usa_ref)

    bufs = ((bua_ref, usa_ref), (bub_ref, usb_ref))
    n_here = u_ref.shape[0] // steps
    x_re, x_im = st_ref[0], st_ref[1]
    for j in range(n_here):
        cur, nxt = bufs[j % 2], bufs[(j + 1) % 2]
        if j + 1 < n_here:
            project(u_ref, (j + 1) * steps, *nxt)
        else:
            project(un_ref, 0, *nxt)
        x_re, x_im = scan(cur[0], x_re, x_im)
        if emit:
            output(*cur, j * steps)
    st_ref[0] = x_re
    st_ref[1] = x_im
    fin_ref[0] = x_re
    fin_ref[1] = x_im


def _ssm_pass(u_tm, wb, a_cat, wc, d_row, init, steps_per_tile, emit):
    n_steps, slot_w = u_tm.shape
    ssm_w = slot_w // SUBLANES
    rows = steps_per_tile * SUBLANES
    n_tiles = n_steps // steps_per_tile
    per_step = SSM_TILES_PER_STEP if n_tiles % SSM_TILES_PER_STEP == 0 else 2
    assert n_tiles % per_step == 0
    st_lanes = a_cat.shape[2]
    r = jnp.arange(rows)
    src = (r % SUBLANES) * steps_per_tile + r // SUBLANES
    perm = (src[:, None] == r[None, :]).astype(BF16)
    perm_t = perm.T
    fin_shape = jax.ShapeDtypeStruct((2, SUBLANES, st_lanes), F32)
    fin_spec = pl.BlockSpec((2, SUBLANES, st_lanes), lambda i: (0, 0, 0))
    scratch = [pltpu.VMEM((2, rows, st_lanes), F32), pltpu.VMEM((2, rows, st_lanes), F32),
               pltpu.VMEM((rows, ssm_w), BF16), pltpu.VMEM((rows, ssm_w), BF16),
               pltpu.VMEM((2, SUBLANES, st_lanes), F32)]
    if emit:
        out_shape = (jax.ShapeDtypeStruct((n_steps, slot_w), BF16), fin_shape)
        out_specs = (pl.BlockSpec((per_step * steps_per_tile, slot_w), lambda i: (i, 0)), fin_spec)
        scratch.append(pltpu.VMEM((rows, ssm_w), BF16))
    else:
        out_shape = fin_shape
        out_specs = fin_spec
    return pl.pallas_call(
        functools.partial(_ssm_kernel, emit=emit),
        out_shape=out_shape,
        grid=(n_tiles // per_step,),
        in_specs=[pl.BlockSpec((per_step * steps_per_tile, slot_w), lambda i: (i, 0)),
                  pl.BlockSpec((steps_per_tile, slot_w),
                               lambda i: (jnp.minimum(per_step * (i + 1), n_tiles - 1), 0)),
                  pl.BlockSpec(perm.shape, lambda i: (0, 0)),
                  pl.BlockSpec(perm.shape, lambda i: (0, 0)),
                  pl.BlockSpec(wb.shape, lambda i: (0, 0, 0)),
                  pl.BlockSpec(a_cat.shape, lambda i: (0, 0, 0)),
                  pl.BlockSpec(wc.shape, lambda i: (0, 0, 0)),
                  pl.BlockSpec(d_row.shape, lambda i: (0, 0)),
                  pl.BlockSpec(init.shape, lambda i: (0, 0, 0))],
        out_specs=out_specs,
        scratch_shapes=scratch,
        compiler_params=_cparams(("arbitrary",)),
        name="ssm_scan" if emit else "ssm_state",
    )(u_tm, u_tm, perm, perm_t, wb, a_cat, wc, d_row, init)


def _attn_kernel(slope_ref, qt_ref, ka_ref, vt_ref, o_ref, acc_ref, sa_ref, sb_ref):
    hp = pl.program_id(1)
    i = pl.program_id(2)
    tq = qt_ref.shape[2]
    key_row = lax.broadcasted_iota(jnp.int32, (KV_TILE, 1), 0)
    q_lane = lax.broadcasted_iota(jnp.int32, (1, tq), 1)
    causal = key_row <= q_lane
    acc_ref[...] = jnp.zeros_like(acc_ref)
    halves = KV_TILE // MOBA_BLOCK

    def scores(kt, s_ref):
        for hh in range(2):
            s_ref[hh] = _dot(ka_ref[hh, kt], qt_ref[hh])

    def col_max(s_t):
        rows = s_t.shape[0]
        while rows > SUBLANES:
            rows //= 2
            s_t = jnp.maximum(s_t[:rows], s_t[rows:])
        return jnp.max(s_t, axis=0, keepdims=True)

    def step(kt, s_ref, ms, last):
        new_ms = []
        for hh in range(2):
            slope = slope_ref[hp * 2 + hh]
            parts, shifts = [], []
            for d in range(halves):
                rows = slice(d * MOBA_BLOCK, (d + 1) * MOBA_BLOCK)
                s_t = s_ref[hh, rows, :]
                if last:
                    s_t = jnp.where(causal[rows], s_t, NEG_BIG)
                parts.append(s_t)
                shifts.append(slope * (((kt - i) * halves + d) * MOBA_BLOCK).astype(F32))
            m_new = ms[hh]
            for s_t, shift in zip(parts, shifts):
                m_new = jnp.maximum(m_new, col_max(s_t) + shift)
            p = jnp.concatenate([jnp.exp2(s_t - (m_new - shift)) for s_t, shift in zip(parts, shifts)],
                                axis=0).astype(BF16)
            acc_ref[hh] = acc_ref[hh] * jnp.exp2(ms[hh] - m_new) + _dot(vt_ref[hh, kt], p)
            new_ms.append(m_new)
        return tuple(new_ms)

    m0 = jnp.full((1, tq), 0.1 * NEG_BIG, F32)
    last_kt = i

    def run_tiles(first, count, ms, ends_sequence):
        bufs = (sa_ref, sb_ref)
        for j in range(count):
            if j + 1 < count or not ends_sequence:
                scores(first + j + 1, bufs[(j + 1) % 2])
            ms = step(first + j, bufs[j % 2], ms, ends_sequence and j + 1 == count)
        return ms

    scores(0, sa_ref)
    ms = lax.fori_loop(0, last_kt // TILES_PER_ITER,
                       lambda q, ms: run_tiles(q * TILES_PER_ITER, TILES_PER_ITER, ms, False), (m0, m0))
    rem = last_kt % TILES_PER_ITER
    for k in range(TILES_PER_ITER):
        @pl.when(rem == k)
        def _(k=k):
            run_tiles(last_kt - k, k + 1, ms, True)


    outs = []
    for hh in range(2):
        acc = acc_ref[hh]
        outs.append(acc[:HEAD_DIM] / acc[HEAD_DIM:HEAD_DIM + 1])
    o_ref[...] = jnp.transpose(jnp.concatenate(outs, axis=0)).astype(o_ref.dtype)


def _attention(qt, ka, vt):
    bsz, n_heads, _, s = qt.shape
    n_kt = ka.shape[2]
    tq = KV_TILE
    slopes = jnp.asarray([sl * LOG2E for sl in _alibi_slopes(n_heads)], F32)
    return pl.pallas_call(
        _attn_kernel,
        out_shape=jax.ShapeDtypeStruct((bsz, s, n_heads * HEAD_DIM), BF16),
        grid=(bsz, n_heads // 2, n_kt),
        in_specs=[pl.BlockSpec(memory_space=pltpu.SMEM),
                  pl.BlockSpec((None, 2, LANES, tq), lambda b, hp, i: (b, hp, 0, i)),
                  pl.BlockSpec((None, 2, n_kt, KV_TILE, LANES), lambda b, hp, i: (b, hp, 0, 0, 0)),
                  pl.BlockSpec((None, 2, n_kt, V_ROWS, KV_TILE), lambda b, hp, i: (b, hp, 0, 0, 0))],
        out_specs=pl.BlockSpec((None, tq, LANES), lambda b, hp, i: (b, i, hp)),
        scratch_shapes=[pltpu.VMEM((2, V_ROWS, tq), F32),
                        pltpu.VMEM((2, KV_TILE, tq), F32),
                        pltpu.VMEM((2, KV_TILE, tq), F32)],
        compiler_params=_cparams(("arbitrary", "arbitrary", "arbitrary")),
        name="attn",
    )(slopes, qt, ka, vt)


def _rms(y, g):
    return y * lax.rsqrt(jnp.mean(y * y, axis=-1, keepdims=True) + NORM_EPS) * g


def _merge_kernel(x_ref, z_ref, o_ref, sg_ref, mod_ref, wa_ref, wb_ref, wo_ref, wout_ref, gpost_ref, out_ref):
    d = x_ref.shape[1]
    z = z_ref[...]
    y_a = _dot(z, wa_ref[...]) * jax.nn.sigmoid(_dot(z, wb_ref[...]))
    y_b = _dot(o_ref[...], wo_ref[...])
    merged = sg_ref[:, :d].astype(F32) * y_a + sg_ref[:, d:].astype(F32) * y_b
    y = _dot(merged.astype(BF16), wout_ref[...])
    out_ref[...] = x_ref[...] + mod_ref[2:3, :] * _rms(y, gpost_ref[...])


def _merge(x, z_tm, o, sg, mod, wa, wb, wo, wout, g_post, tm):
    bsz, s, d = x.shape
    n_t = s // tm
    half_t = n_t // 2
    ssm_w = wa.shape[0]
    full = lambda a: pl.BlockSpec(a.shape, lambda b, i: (0,) * a.ndim, pipeline_mode=pl.Buffered(1))
    return pl.pallas_call(
        _merge_kernel,
        out_shape=jax.ShapeDtypeStruct((bsz, s, d), F32),
        grid=(bsz, n_t),
        in_specs=[pl.BlockSpec((None, tm, d), lambda b, i: (b, i, 0)),
                  pl.BlockSpec((tm, ssm_w), lambda b, i: (i % half_t, (i // half_t) * bsz + b)),
                  pl.BlockSpec((None, tm, o.shape[2]), lambda b, i: (b, i, 0)),
                  pl.BlockSpec((None, tm, sg.shape[2]), lambda b, i: (b, i, 0)),
                  pl.BlockSpec((None, N_MOD, d), lambda b, i: (b, 0, 0)),
                  full(wa), full(wb), full(wo), full(wout), full(g_post)],
        out_specs=pl.BlockSpec((None, tm, d), lambda b, i: (b, i, 0)),
        compiler_params=_cparams(("arbitrary", "arbitrary")),
        name="merge",
    )(x, z_tm, o, sg, mod, wa, wb, wo, wout, g_post)


def _ffn_kernel(x_ref, mod_ref, gpre_ref, wg_ref, wu_ref, wd_ref, gpost_ref, out_ref, *, n_chunks):
    x = x_ref[...]
    hn = _rms(x, gpre_ref[...]) * (1.0 + mod_ref[4:5, :]) + mod_ref[3:4, :]
    hb = hn.astype(BF16)
    ff = wg_ref.shape[1]
    cw = ff // n_chunks
    y = None
    for c in range(n_chunks):
        cols = slice(c * cw, (c + 1) * cw)
        gate = _dot(hb, wg_ref[:, cols])
        up = _dot(hb, wu_ref[:, cols])
        mid = (gate * jax.nn.sigmoid(gate) * up).astype(BF16)
        part = _dot(mid, wd_ref[cols, :])
        y = part if y is None else y + part
    out_ref[...] = x + mod_ref[5:6, :] * _rms(y, gpost_ref[...])


def _ffn(x, mod, g_pre, wg, wu, wd, g_post, tm):
    bsz, s, d = x.shape
    ff = wg.shape[1]
    n_chunks = 2 if (ff // 2) % LANES == 0 else 1
    full = lambda a: pl.BlockSpec(a.shape, lambda b, i: (0,) * a.ndim, pipeline_mode=pl.Buffered(1))
    return pl.pallas_call(
        functools.partial(_ffn_kernel, n_chunks=n_chunks),
        out_shape=jax.ShapeDtypeStruct((bsz, s, d), F32),
        grid=(bsz, s // tm),
        in_specs=[pl.BlockSpec((None, tm, d), lambda b, i: (b, i, 0)),
                  pl.BlockSpec((None, N_MOD, d), lambda b, i: (b, 0, 0)),
                  full(g_pre), full(wg), full(wu), full(wd), full(g_post)],
        out_specs=pl.BlockSpec((None, tm, d), lambda b, i: (b, i, 0)),
        compiler_params=_cparams(("arbitrary", "arbitrary")),
        name="ffn",
    )(x, mod, g_pre, wg, wu, wd, g_post)


def _layer(x, mod, g_pre_mix, g_post_mix, w_in, ssm_a_re, ssm_a_im, ssm_log_dt, ssm_b_re, ssm_b_im,
           ssm_c_re, ssm_c_im, ssm_d, w_glu_a, w_glu_b, w_attn_out, w_out,
           g_pre_ffn, g_post_ffn, w_ff_gate, w_ff_up, w_ff_down):
    bsz, s, d = x.shape
    n_groups, n_state = ssm_a_re.shape
    ssm_w = n_groups * SSM_GROUP
    width = w_attn_out.shape[0]
    n_heads = width // HEAD_DIM
    assert 2 * bsz == SUBLANES, "SSM scan packs 2 sequence halves x batch into the 8 sublanes"
    tm = min(1024, s // 2)
    steps_per_tile = min(64, s // 2)
    assert s % (2 * tm) == 0 and tm % MOBA_BLOCK == 0 and (s // 2) % steps_per_tile == 0

    mod3 = mod.reshape(bsz, N_MOD, d)
    row = lambda g: g.reshape(1, -1)

    ab_re, ab_im, bbt_re, bbt_im = _ssm_prep(ssm_a_re, ssm_a_im, ssm_log_dt, ssm_b_re, ssm_b_im)
    wb = jnp.concatenate([_block_diag_tiles(bbt_re), _block_diag_tiles(bbt_im)], axis=-1).astype(BF16)
    wc = jnp.concatenate([_block_diag_tiles(jnp.transpose(ssm_c_re, (0, 2, 1))),
                          _block_diag_tiles(-jnp.transpose(ssm_c_im, (0, 2, 1)))], axis=1).astype(BF16)
    a_cat = jnp.stack([ab_re.reshape(-1), ab_im.reshape(-1)], axis=0)
    a_cat = jnp.broadcast_to(a_cat[:, None, :], (2, SUBLANES, a_cat.shape[1]))
    d_row = ssm_d.reshape(1, ssm_w)

    u_tm, qt, ka, vt, sg = _proj(x, mod3, row(g_pre_mix), w_in, n_heads, ssm_w, tm)

    zero_state = jnp.zeros((2, SUBLANES, n_groups * n_state), F32)
    first = _ssm_pass(u_tm, wb, a_cat, wc, d_row, zero_state, steps_per_tile, emit=False)
    init = jnp.concatenate([jnp.zeros_like(first[:, :bsz]), first[:, :bsz]], axis=1)
    z_tm, _ = _ssm_pass(u_tm, wb, a_cat, wc, d_row, init, steps_per_tile, emit=True)

    o = _attention(qt, ka, vt)

    x1 = _merge(x, z_tm, o, sg, mod3, w_glu_a.astype(BF16), w_glu_b.astype(BF16),
                w_attn_out.astype(BF16), w_out.astype(BF16), row(g_post_mix), tm)
    return _ffn(x1, mod3, row(g_pre_ffn), w_ff_gate.astype(BF16), w_ff_up.astype(BF16),
                w_ff_down.astype(BF16), row(g_post_ffn), tm)


def kernel(x, c, w_ada, b_ada, g_pre_mix, g_post_mix, w_in, ssm_a_re, ssm_a_im, ssm_log_dt, ssm_b_re, ssm_b_im, ssm_c_re, ssm_c_im, ssm_d, w_glu_a, w_glu_b, w_attn_out, w_out, g_pre_ffn, g_post_ffn, w_ff_gate, w_ff_up, w_ff_down):
    depth = w_ada.shape[0]
    for l in range(depth):
        mod = _adaln(c, w_ada[l], b_ada[l])
        x = _layer(x, mod, g_pre_mix[l], g_post_mix[l], w_in[l], ssm_a_re[l], ssm_a_im[l], ssm_log_dt[l],
                   ssm_b_re[l], ssm_b_im[l], ssm_c_re[l], ssm_c_im[l], ssm_d[l], w_glu_a[l], w_glu_b[l],
                   w_attn_out[l], w_out[l], g_pre_ffn[l], g_post_ffn[l], w_ff_gate[l], w_ff_up[l], w_ff_down[l])
    return x
```

```python
import functools
import math

import jax
import jax.numpy as jnp
from jax import lax
from jax.experimental import pallas as pl
from jax.experimental.pallas import tpu as pltpu

F32 = jnp.float32
BF16 = jnp.bfloat16

SSM_GROUP = 16
SSM_STATE = 64
HEAD_DIM = 64
MOBA_BLOCK = 256
MOBA_TOPK = 3
N_MOD = 6
NORM_EPS = 1e-6
LANES = 128
SUBLANES = 8
GROUPS_PER_TILE = LANES // SSM_GROUP
NEG_BIG = -1e30
LOG2E = math.log2(math.e)
MASK_FEATS = 32
V_ROWS = HEAD_DIM + 16
KV_TILE = 2 * MOBA_BLOCK
SSM_TILES_PER_STEP = 4
TILES_PER_ITER = 8
VMEM_LIMIT = 56 * 1024 * 1024


def _cparams(sem):
    return pltpu.CompilerParams(dimension_semantics=sem, vmem_limit_bytes=VMEM_LIMIT)


def _dot(a, b):
    return jnp.dot(a, b, preferred_element_type=F32)


def _dot_split3(a, b):
    a_hi = a.astype(BF16)
    b_hi = b.astype(BF16)
    a_lo = (a - a_hi.astype(F32)).astype(BF16)
    b_lo = (b - b_hi.astype(F32)).astype(BF16)
    return _dot(a_hi, b_hi) + (_dot(a_hi, b_lo) + _dot(a_lo, b_hi))


def _dot_nt(a, b):
    return lax.dot_general(a, b, (((1,), (1,)), ((), ())), preferred_element_type=F32)


def _adaln_kernel(c_ref, w_ref, b_ref, o_ref):
    c = c_ref[...]
    act = c * jax.nn.sigmoid(c)
    o_ref[...] = jnp.dot(act, w_ref[...], preferred_element_type=F32,
                         precision=lax.Precision.HIGHEST) + b_ref[...]


def _adaln(c, w_ada, b_ada):
    bsz, d = c.shape
    n = w_ada.shape[1]
    tn = n // N_MOD
    return pl.pallas_call(
        _adaln_kernel,
        out_shape=jax.ShapeDtypeStruct((bsz, n), F32),
        grid=(n // tn,),
        in_specs=[pl.BlockSpec((bsz, d), lambda j: (0, 0)),
                  pl.BlockSpec((d, tn), lambda j: (0, j)),
                  pl.BlockSpec((1, tn), lambda j: (0, j))],
        out_specs=pl.BlockSpec((bsz, tn), lambda j: (0, j)),
        compiler_params=_cparams(("arbitrary",)),
        name="adaln",
    )(c, w_ada, b_ada.reshape(1, n))


def _ssm_prep_kernel(are_ref, aim_ref, ldt_ref, bre_ref, bim_ref,
                     abre_ref, abim_ref, bbre_ref, bbim_ref):
    a_re = are_ref[...]
    a_im = aim_ref[...]
    dt = jnp.exp(ldt_ref[...])
    mag = jnp.exp(dt * a_re)
    ab_re = mag * jnp.cos(dt * a_im)
    ab_im = mag * jnp.sin(dt * a_im)
    den = a_re * a_re + a_im * a_im
    nr = ab_re - 1.0
    co_re = (nr * a_re + ab_im * a_im) / den
    co_im = (ab_im * a_re - nr * a_im) / den
    abre_ref[...] = ab_re
    abim_ref[...] = ab_im
    b_re = bre_ref[...]
    b_im = bim_ref[...]
    bbre_ref[...] = co_re[:, None, :] * b_re - co_im[:, None, :] * b_im
    bbim_ref[...] = co_re[:, None, :] * b_im + co_im[:, None, :] * b_re


def _ssm_prep(a_re, a_im, log_dt, b_re, b_im):
    g, p = a_re.shape
    c = b_re.shape[-1]
    bt_re = jnp.transpose(b_re, (0, 2, 1))
    bt_im = jnp.transpose(b_im, (0, 2, 1))
    return pl.pallas_call(
        _ssm_prep_kernel,
        out_shape=(jax.ShapeDtypeStruct((g, p), F32), jax.ShapeDtypeStruct((g, p), F32),
                   jax.ShapeDtypeStruct((g, c, p), F32), jax.ShapeDtypeStruct((g, c, p), F32)),
        name="ssm_prep",
    )(a_re, a_im, log_dt.reshape(g, 1), bt_re, bt_im)


def _block_diag_tiles(w):
    g, r, cc = w.shape
    nt = g // GROUPS_PER_TILE
    w = w.reshape(nt, GROUPS_PER_TILE, r, cc)
    eye = jnp.eye(GROUPS_PER_TILE, dtype=w.dtype)
    return jnp.einsum('sgrc,gh->sgrhc', w, eye).reshape(nt, GROUPS_PER_TILE * r, GROUPS_PER_TILE * cc)


def _alibi_slopes(n_heads):
    return [2.0 ** (-8.0 * (h + 1) / n_heads) for h in range(n_heads)]


def _proj_kernel(x_ref, mod_ref, gpre_ref, wu_ref, wqt_ref, wk_ref, wvt_ref, wg_ref, perm_ref, kbias_ref,
                 u_ref, qt_ref, ka_ref, vt_ref, sg_ref, kmt_ref, *, n_heads, n_blk):
    i = pl.program_id(1)
    tm = x_ref.shape[0]
    width = n_heads * HEAD_DIM
    blk_per_tile = tm // MOBA_BLOCK

    @pl.when(i == 0)
    def _():
        kmt_ref[...] = jnp.zeros_like(kmt_ref)

    x = x_ref[...]
    ms = jnp.mean(x * x, axis=-1, keepdims=True)
    gain = gpre_ref[...] * (1.0 + mod_ref[1:2, :])
    hn = x * lax.rsqrt(ms + NORM_EPS) * gain + mod_ref[0:1, :]
    hb = hn.astype(BF16)

    q_t = _dot_nt(wqt_ref[...], hb)
    k = _dot(hb, wk_ref[...])

    head_of_lane = lax.broadcasted_iota(jnp.int32, (n_heads, width), 1) // HEAD_DIM
    head_mask = (head_of_lane == lax.broadcasted_iota(jnp.int32, (n_heads, width), 0)).astype(F32)
    for lb in range(blk_per_tile):
        km = jnp.mean(k[lb * MOBA_BLOCK:(lb + 1) * MOBA_BLOCK, :], axis=0, keepdims=True)
        row0 = pl.multiple_of((i * blk_per_tile + lb) * n_heads, n_heads)
        kmt_ref[pl.ds(row0, n_heads), :] = km * head_mask

    gate = _dot_split3(kmt_ref[...], q_t)
    u_ref[...] = _dot(hb, wu_ref[...]).astype(BF16)
    v_t = _dot_nt(wvt_ref[...], hb)
    gate = gate.reshape(n_blk, n_heads, tm)
    n_iota = lax.broadcasted_iota(jnp.int32, (n_blk, n_heads, tm), 0)
    own = i * blk_per_tile + lax.broadcasted_iota(jnp.int32, (n_blk, n_heads, tm), 2) // MOBA_BLOCK
    neg_inf = jnp.float32(-jnp.inf)
    cur = jnp.where(n_iota < own, gate, neg_inf)
    sel = jnp.zeros((n_blk, n_heads, tm), F32)
    for _ in range(MOBA_TOPK):
        best = jnp.max(cur, axis=0, keepdims=True)
        idx = jnp.min(jnp.where(cur == best, n_iota, n_blk), axis=0, keepdims=True)
        hit = (n_iota == idx) & (best > neg_inf)
        sel = jnp.where(hit, 1.0, sel)
        cur = jnp.where(hit, neg_inf, cur)
    unsel = 1.0 - sel - (n_iota == own).astype(F32)
    unsel = unsel.reshape(n_blk * n_heads, tm).astype(BF16)
    mask_rows = _dot(perm_ref[...], unsel) * NEG_BIG
    sg_ref[...] = jax.nn.sigmoid(_dot(hb, wg_ref[...])).astype(BF16)

    sub8 = lax.broadcasted_iota(jnp.int32, (SUBLANES, tm), 0)
    ones3 = (sub8 < 3).astype(F32)
    ones1 = (sub8 < 1).astype(F32)
    zeros = lambda r: jnp.zeros((r, tm), F32)
    lane = lax.broadcasted_iota(jnp.int32, (tm, LANES), 1)
    row = lax.broadcasted_iota(jnp.int32, (tm, LANES), 0)
    own_row = i * blk_per_tile + row // MOBA_BLOCK
    scale = HEAD_DIM ** -0.5 * LOG2E
    slopes = _alibi_slopes(n_heads)
    pad_feats = LANES - HEAD_DIM - MASK_FEATS - SUBLANES
    for h in range(n_heads):
        rows = slice(h * HEAD_DIM, (h + 1) * HEAD_DIM)
        tile = slice((h // 2) * LANES, (h // 2 + 1) * LANES)
        q_rows = q_t[rows] * scale
        m_rows = mask_rows[h * n_blk:(h + 1) * n_blk]
        if n_blk < MASK_FEATS:
            m_rows = jnp.concatenate([m_rows, zeros(MASK_FEATS - n_blk)], axis=0)
        if h % 2 == 0:
            qt = jnp.concatenate([q_rows, m_rows, ones3, zeros(pad_feats)], axis=0)
            data_lanes = lane < HEAD_DIM
            feat0 = HEAD_DIM
        else:
            qt = jnp.concatenate([m_rows, ones3, zeros(pad_feats), q_rows], axis=0)
            data_lanes = lane >= HEAD_DIM
            feat0 = 0
        qt_ref[h] = qt.astype(BF16)
        feats = jnp.where(lane == feat0 + own_row, 1.0, kbias_ref[h])
        ka = jnp.where(data_lanes, k[:, tile], feats).astype(BF16)
        vt = jnp.concatenate([v_t[rows], ones1, zeros(V_ROWS - HEAD_DIM - SUBLANES)], axis=0).astype(BF16)
        for kt in range(tm // KV_TILE):
            ka_ref[h, kt] = ka[kt * KV_TILE:(kt + 1) * KV_TILE]
            vt_ref[h, kt] = vt[:, kt * KV_TILE:(kt + 1) * KV_TILE]


def _leading_8_bits(x):
    c = x * jnp.float32(2 ** 16 + 1)
    return c - (c - x)


def _alibi_key_features(n_heads, tm):
    pos = (jnp.arange(tm) % MOBA_BLOCK).astype(F32)
    lane = jnp.arange(LANES)
    out = []
    for h, slope in enumerate(_alibi_slopes(n_heads)):
        feat0 = (HEAD_DIM if h % 2 == 0 else 0) + MASK_FEATS
        bias = (slope * LOG2E) * pos
        hi = _leading_8_bits(bias)
        mid = _leading_8_bits(bias - hi)
        lo = bias - hi - mid
        feats = jnp.zeros((tm, LANES), F32)
        for j, term in enumerate((hi, mid, lo)):
            feats = jnp.where(lane[None, :] == feat0 + j, term[:, None], feats)
        out.append(feats)
    return jnp.stack(out, axis=0)


def _row_permutation(n_blk, n_heads):
    r = jnp.arange(n_heads * n_blk)
    src = (r % n_blk) * n_heads + r // n_blk
    return (src[:, None] == jnp.arange(n_blk * n_heads)[None, :]).astype(BF16)


def _proj(x, mod, g_pre, w_in, n_heads, ssm_w, tm):
    bsz, s, d = x.shape
    n_t = s // tm
    half_t = n_t // 2
    n_blk = s // MOBA_BLOCK
    width = n_heads * HEAD_DIM
    assert n_blk <= MASK_FEATS and tm % KV_TILE == 0
    c = [0, ssm_w, ssm_w + width, ssm_w + 2 * width, ssm_w + 3 * width, w_in.shape[1]]
    gate_w = c[5] - c[4]
    wu = w_in[:, c[0]:c[1]].astype(BF16)
    wqt = w_in[:, c[1]:c[2]].T.astype(BF16)
    wk = w_in[:, c[2]:c[3]].astype(BF16)
    wvt = w_in[:, c[3]:c[4]].T.astype(BF16)
    wg = w_in[:, c[4]:c[5]].astype(BF16)
    perm = _row_permutation(n_blk, n_heads)
    kbias = _alibi_key_features(n_heads, tm)
    full = lambda a: pl.BlockSpec(a.shape, lambda b, i: (0,) * a.ndim, pipeline_mode=pl.Buffered(1))
    kern = functools.partial(_proj_kernel, n_heads=n_heads, n_blk=n_blk)
    return pl.pallas_call(
        kern,
        out_shape=(jax.ShapeDtypeStruct((s // 2, 2 * bsz * ssm_w), BF16),
                   jax.ShapeDtypeStruct((bsz, n_heads, LANES, s), BF16),
                   jax.ShapeDtypeStruct((bsz, n_heads, s // KV_TILE, KV_TILE, LANES), BF16),
                   jax.ShapeDtypeStruct((bsz, n_heads, s // KV_TILE, V_ROWS, KV_TILE), BF16),
                   jax.ShapeDtypeStruct((bsz, s, gate_w), BF16)),
        grid=(bsz, n_t),
        in_specs=[pl.BlockSpec((None, tm, d), lambda b, i: (b, i, 0)),
                  pl.BlockSpec((None, N_MOD, d), lambda b, i: (b, 0, 0)),
                  full(g_pre), full(wu), full(wqt), full(wk), full(wvt), full(wg), full(perm), full(kbias)],
        out_specs=(pl.BlockSpec((tm, ssm_w), lambda b, i: (i % half_t, (i // half_t) * bsz + b)),
                   pl.BlockSpec((None, n_heads, LANES, tm), lambda b, i: (b, 0, 0, i)),
                   pl.BlockSpec((None, n_heads, tm // KV_TILE, KV_TILE, LANES), lambda b, i: (b, 0, i, 0, 0)),
                   pl.BlockSpec((None, n_heads, tm // KV_TILE, V_ROWS, KV_TILE), lambda b, i: (b, 0, i, 0, 0)),
                   pl.BlockSpec((None, tm, gate_w), lambda b, i: (b, i, 0))),
        scratch_shapes=[pltpu.VMEM((n_blk * n_heads, width), F32)],
        compiler_params=_cparams(("arbitrary", "arbitrary")),
        name="proj",
    )(x, mod, g_pre, wu, wqt, wk, wvt, wg, perm, kbias)


def _gelu_tanh(x):
    c = math.sqrt(2.0 / math.pi)
    return 0.5 * x * (1.0 + jnp.tanh(c * (x + 0.044715 * (x * x * x))))


def _ssm_kernel(u_ref, un_ref, perm_ref, permt_ref, wb_ref, a_ref, wc_ref, d_ref, init_ref, *rest, emit):
    if emit:
        z_ref, fin_ref, bua_ref, bub_ref, usa_ref, usb_ref, st_ref, zs_ref = rest
    else:
        fin_ref, bua_ref, bub_ref, usa_ref, usb_ref, st_ref = rest
    i = pl.program_id(0)
    steps = un_ref.shape[0]
    ssm_w = u_ref.shape[1] // SUBLANES
    n_tiles = wb_ref.shape[0]
    st_w = wb_ref.shape[2] // 2

    def project(src_ref, t0, bu_ref, us_ref):
        u_nat = jnp.concatenate([src_ref[t0:t0 + steps, s * ssm_w:(s + 1) * ssm_w]
                                 for s in range(SUBLANES)], axis=0)
        us_ref[...] = _dot(perm_ref[...], u_nat).astype(BF16)
        for sg in range(n_tiles):
            r = _dot(us_ref[:, sg * LANES:(sg + 1) * LANES], wb_ref[sg])
            bu_ref[0, :, sg * st_w:(sg + 1) * st_w] = r[:, :st_w]
            bu_ref[1, :, sg * st_w:(sg + 1) * st_w] = r[:, st_w:]

    def scan(bu_ref, x_re, x_im):
        for t in range(steps):
            rows = slice(t * SUBLANES, (t + 1) * SUBLANES)
            a_re = a_ref[0]
            a_im = a_ref[1]
            n_re = a_re * x_re - a_im * x_im + bu_ref[0, rows, :]
            n_im = a_re * x_im + a_im * x_re + bu_ref[1, rows, :]
            if emit:
                bu_ref[0, rows, :] = n_re
                bu_ref[1, rows, :] = n_im
            x_re, x_im = n_re, n_im
        return x_re, x_im

    def output(bu_ref, us_ref, t0):
        for sg in range(n_tiles):
            xr = bu_ref[0, :, sg * st_w:(sg + 1) * st_w].astype(BF16)
            xi = bu_ref[1, :, sg * st_w:(sg + 1) * st_w].astype(BF16)
            y = _dot(xr, wc_ref[sg, :st_w, :]) + _dot(xi, wc_ref[sg, st_w:, :])
            lanes = slice(sg * LANES, (sg + 1) * LANES)
            y = y + d_ref[:, lanes] * us_ref[:, lanes].astype(F32)
            zs_ref[:, lanes] = _gelu_tanh(y).astype(BF16)
        z_nat = _dot(permt_ref[...], zs_ref[...]).astype(BF16)
        for s in range(SUBLANES):
            z_ref[t0:t0 + steps, s * ssm_w:(s + 1) * ssm_w] = z_nat[s * steps:(s + 1) * steps]

    @pl.when(i == 0)
    def _():
        st_ref[...] = init_ref[...]
        project(u_ref, 0, bua_ref, usa_ref)

    bufs = ((bua_ref, usa_ref), (bub_ref, usb_ref))
    n_here = u_ref.shape[0] // steps
    x_re, x_im = st_ref[0], st_ref[1]
    for j in range(n_here):
        cur, nxt = bufs[j % 2], bufs[(j + 1) % 2]
        if j + 1 < n_here:
            project(u_ref, (j + 1) * steps, *nxt)
        else:
            project(un_ref, 0, *nxt)
        x_re, x_im = scan(cur[0], x_re, x_im)
        if emit:
            output(*cur, j * steps)
    st_ref[0] = x_re
    st_ref[1] = x_im
    fin_ref[0] = x_re
    fin_ref[1] = x_im


def _ssm_pass(u_tm, wb, a_cat, wc, d_row, init, steps_per_tile, emit):
    n_steps, slot_w = u_tm.shape
    ssm_w = slot_w // SUBLANES
    rows = steps_per_tile * SUBLANES
    n_tiles = n_steps // steps_per_tile
    per_step = SSM_TILES_PER_STEP if n_tiles % SSM_TILES_PER_STEP == 0 else 2
    assert n_tiles % per_step == 0
    st_lanes = a_cat.shape[2]
    r = jnp.arange(rows)
    src = (r % SUBLANES) * steps_per_tile + r // SUBLANES
    perm = (src[:, None] == r[None, :]).astype(BF16)
    perm_t = perm.T
    fin_shape = jax.ShapeDtypeStruct((2, SUBLANES, st_lanes), F32)
    fin_spec = pl.BlockSpec((2, SUBLANES, st_lanes), lambda i: (0, 0, 0))
    scratch = [pltpu.VMEM((2, rows, st_lanes), F32), pltpu.VMEM((2, rows, st_lanes), F32),
               pltpu.VMEM((rows, ssm_w), BF16), pltpu.VMEM((rows, ssm_w), BF16),
               pltpu.VMEM((2, SUBLANES, st_lanes), F32)]
    if emit:
        out_shape = (jax.ShapeDtypeStruct((n_steps, slot_w), BF16), fin_shape)
        out_specs = (pl.BlockSpec((per_step * steps_per_tile, slot_w), lambda i: (i, 0)), fin_spec)
        scratch.append(pltpu.VMEM((rows, ssm_w), BF16))
    else:
        out_shape = fin_shape
        out_specs = fin_spec
    return pl.pallas_call(
        functools.partial(_ssm_kernel, emit=emit),
        out_shape=out_shape,
        grid=(n_tiles // per_step,),
        in_specs=[pl.BlockSpec((per_step * steps_per_tile, slot_w), lambda i: (i, 0)),
                  pl.BlockSpec((steps_per_tile, slot_w),
                               lambda i: (jnp.minimum(per_step * (i + 1), n_tiles - 1), 0)),
                  pl.BlockSpec(perm.shape, lambda i: (0, 0)),
                  pl.BlockSpec(perm.shape, lambda i: (0, 0)),
                  pl.BlockSpec(wb.shape, lambda i: (0, 0, 0)),
                  pl.BlockSpec(a_cat.shape, lambda i: (0, 0, 0)),
                  pl.BlockSpec(wc.shape, lambda i: (0, 0, 0)),
                  pl.BlockSpec(d_row.shape, lambda i: (0, 0)),
                  pl.BlockSpec(init.shape, lambda i: (0, 0, 0))],
        out_specs=out_specs,
        scratch_shapes=scratch,
        compiler_params=_cparams(("arbitrary",)),
        name="ssm_scan" if emit else "ssm_state",
    )(u_tm, u_tm, perm, perm_t, wb, a_cat, wc, d_row, init)


def _attn_kernel(slope_ref, qt_ref, ka_ref, vt_ref, o_ref, acc_ref, sa_ref, sb_ref):
    hp = pl.program_id(1)
    i = pl.program_id(2)
    tq = qt_ref.shape[2]
    key_row = lax.broadcasted_iota(jnp.int32, (KV_TILE, 1), 0)
    q_lane = lax.broadcasted_iota(jnp.int32, (1, tq), 1)
    causal = key_row <= q_lane
    acc_ref[...] = jnp.zeros_like(acc_ref)
    halves = KV_TILE // MOBA_BLOCK

    def scores(kt, s_ref):
        for hh in range(2):
            s_ref[hh] = _dot(ka_ref[hh, kt], qt_ref[hh])

    def col_max(s_t):
        rows = s_t.shape[0]
        while rows > SUBLANES:
            rows //= 2
            s_t = jnp.maximum(s_t[:rows], s_t[rows:])
        return jnp.max(s_t, axis=0, keepdims=True)

    def step(kt, s_ref, ms, last):
        new_ms = []
        for hh in range(2):
            slope = slope_ref[hp * 2 + hh]
            parts, shifts = [], []
            for d in range(halves):
                rows = slice(d * MOBA_BLOCK, (d + 1) * MOBA_BLOCK)
                s_t = s_ref[hh, rows, :]
                if last:
                    s_t = jnp.where(causal[rows], s_t, NEG_BIG)
                parts.append(s_t)
                shifts.append(slope * (((kt - i) * halves + d) * MOBA_BLOCK).astype(F32))
            m_new = ms[hh]
            for s_t, shift in zip(parts, shifts):
                m_new = jnp.maximum(m_new, col_max(s_t) + shift)
            p = jnp.concatenate([jnp.exp2(s_t - (m_new - shift)) for s_t, shift in zip(parts, shifts)],
                                axis=0).astype(BF16)
            acc_ref[hh] = acc_ref[hh] * jnp.exp2(ms[hh] - m_new) + _dot(vt_ref[hh, kt], p)
            new_ms.append(m_new)
        return tuple(new_ms)

    m0 = jnp.full((1, tq), 0.1 * NEG_BIG, F32)
    last_kt = i

    def run_tiles(first, count, ms, ends_sequence):
        bufs = (sa_ref, sb_ref)
        for j in range(count):
            if j + 1 < count or not ends_sequence:
                scores(first + j + 1, bufs[(j + 1) % 2])
            ms = step(first + j, bufs[j % 2], ms, ends_sequence and j + 1 == count)
        return ms

    scores(0, sa_ref)
    ms = lax.fori_loop(0, last_kt // TILES_PER_ITER,
                       lambda q, ms: run_tiles(q * TILES_PER_ITER, TILES_PER_ITER, ms, False), (m0, m0))
    rem = last_kt % TILES_PER_ITER
    for k in range(TILES_PER_ITER):
        @pl.when(rem == k)
        def _(k=k):
            run_tiles(last_kt - k, k + 1, ms, True)


    outs = []
    for hh in range(2):
        acc = acc_ref[hh]
        outs.append(acc[:HEAD_DIM] / acc[HEAD_DIM:HEAD_DIM + 1])
    o_ref[...] = jnp.transpose(jnp.concatenate(outs, axis=0)).astype(o_ref.dtype)


def _attention(qt, ka, vt):
    bsz, n_heads, _, s = qt.shape
    n_kt = ka.shape[2]
    tq = KV_TILE
    slopes = jnp.asarray([sl * LOG2E for sl in _alibi_slopes(n_heads)], F32)
    return pl.pallas_call(
        _attn_kernel,
        out_shape=jax.ShapeDtypeStruct((bsz, s, n_heads * HEAD_DIM), BF16),
        grid=(bsz, n_heads // 2, n_kt),
        in_specs=[pl.BlockSpec(memory_space=pltpu.SMEM),
                  pl.BlockSpec((None, 2, LANES, tq), lambda b, hp, i: (b, hp, 0, i)),
                  pl.BlockSpec((None, 2, n_kt, KV_TILE, LANES), lambda b, hp, i: (b, hp, 0, 0, 0)),
                  pl.BlockSpec((None, 2, n_kt, V_ROWS, KV_TILE), lambda b, hp, i: (b, hp, 0, 0, 0))],
        out_specs=pl.BlockSpec((None, tq, LANES), lambda b, hp, i: (b, i, hp)),
        scratch_shapes=[pltpu.VMEM((2, V_ROWS, tq), F32),
                        pltpu.VMEM((2, KV_TILE, tq), F32),
                        pltpu.VMEM((2, KV_TILE, tq), F32)],
        compiler_params=_cparams(("arbitrary", "arbitrary", "arbitrary")),
        name="attn",
    )(slopes, qt, ka, vt)


def _rms(y, g):
    return y * lax.rsqrt(jnp.mean(y * y, axis=-1, keepdims=True) + NORM_EPS) * g


def _merge_kernel(x_ref, z_ref, o_ref, sg_ref, mod_ref, wa_ref, wb_ref, wo_ref, wout_ref, gpost_ref, out_ref):
    d = x_ref.shape[1]
    z = z_ref[...]
    y_a = _dot(z, wa_ref[...]) * jax.nn.sigmoid(_dot(z, wb_ref[...]))
    y_b = _dot(o_ref[...], wo_ref[...])
    merged = sg_ref[:, :d].astype(F32) * y_a + sg_ref[:, d:].astype(F32) * y_b
    y = _dot(merged.astype(BF16), wout_ref[...])
    out_ref[...] = x_ref[...] + mod_ref[2:3, :] * _rms(y, gpost_ref[...])


def _merge(x, z_tm, o, sg, mod, wa, wb, wo, wout, g_post, tm):
    bsz, s, d = x.shape
    n_t = s // tm
    half_t = n_t // 2
    ssm_w = wa.shape[0]
    full = lambda a: pl.BlockSpec(a.shape, lambda b, i: (0,) * a.ndim, pipeline_mode=pl.Buffered(1))
    return pl.pallas_call(
        _merge_kernel,
        out_shape=jax.ShapeDtypeStruct((bsz, s, d), F32),
        grid=(bsz, n_t),
        in_specs=[pl.BlockSpec((None, tm, d), lambda b, i: (b, i, 0)),
                  pl.BlockSpec((tm, ssm_w), lambda b, i: (i % half_t, (i // half_t) * bsz + b)),
                  pl.BlockSpec((None, tm, o.shape[2]), lambda b, i: (b, i, 0)),
                  pl.BlockSpec((None, tm, sg.shape[2]), lambda b, i: (b, i, 0)),
                  pl.BlockSpec((None, N_MOD, d), lambda b, i: (b, 0, 0)),
                  full(wa), full(wb), full(wo), full(wout), full(g_post)],
        out_specs=pl.BlockSpec((None, tm, d), lambda b, i: (b, i, 0)),
        compiler_params=_cparams(("arbitrary", "arbitrary")),
        name="merge",
    )(x, z_tm, o, sg, mod, wa, wb, wo, wout, g_post)


def _ffn_kernel(x_ref, mod_ref, gpre_ref, wg_ref, wu_ref, wd_ref, gpost_ref, out_ref, *, n_chunks):
    x = x_ref[...]
    hn = _rms(x, gpre_ref[...]) * (1.0 + mod_ref[4:5, :]) + mod_ref[3:4, :]
    hb = hn.astype(BF16)
    ff = wg_ref.shape[1]
    cw = ff // n_chunks
    y = None
    for c in range(n_chunks):
        cols = slice(c * cw, (c + 1) * cw)
        gate = _dot(hb, wg_ref[:, cols])
        up = _dot(hb, wu_ref[:, cols])
        mid = (gate * jax.nn.sigmoid(gate) * up).astype(BF16)
        part = _dot(mid, wd_ref[cols, :])
        y = part if y is None else y + part
    out_ref[...] = x + mod_ref[5:6, :] * _rms(y, gpost_ref[...])


def _ffn(x, mod, g_pre, wg, wu, wd, g_post, tm):
    bsz, s, d = x.shape
    ff = wg.shape[1]
    n_chunks = 2 if (ff // 2) % LANES == 0 else 1
    full = lambda a: pl.BlockSpec(a.shape, lambda b, i: (0,) * a.ndim, pipeline_mode=pl.Buffered(1))
    return pl.pallas_call(
        functools.partial(_ffn_kernel, n_chunks=n_chunks),
        out_shape=jax.ShapeDtypeStruct((bsz, s, d), F32),
        grid=(bsz, s // tm),
        in_specs=[pl.BlockSpec((None, tm, d), lambda b, i: (b, i, 0)),
                  pl.BlockSpec((None, N_MOD, d), lambda b, i: (b, 0, 0)),
                  full(g_pre), full(wg), full(wu), full(wd), full(g_post)],
        out_specs=pl.BlockSpec((None, tm, d), lambda b, i: (b, i, 0)),
        compiler_params=_cparams(("arbitrary", "arbitrary")),
        name="ffn",
    )(x, mod, g_pre, wg, wu, wd, g_post)


def _layer(x, mod, g_pre_mix, g_post_mix, w_in, ssm_a_re, ssm_a_im, ssm_log_dt, ssm_b_re, ssm_b_im,
           ssm_c_re, ssm_c_im, ssm_d, w_glu_a, w_glu_b, w_attn_out, w_out,
           g_pre_ffn, g_post_ffn, w_ff_gate, w_ff_up, w_ff_down):
    bsz, s, d = x.shape
    n_groups, n_state = ssm_a_re.shape
    ssm_w = n_groups * SSM_GROUP
    width = w_attn_out.shape[0]
    n_heads = width // HEAD_DIM
    assert 2 * bsz == SUBLANES, "SSM scan packs 2 sequence halves x batch into the 8 sublanes"
    tm = min(1024, s // 2)
    steps_per_tile = min(64, s // 2)
    assert s % (2 * tm) == 0 and tm % MOBA_BLOCK == 0 and (s // 2) % steps_per_tile == 0

    mod3 = mod.reshape(bsz, N_MOD, d)
    row = lambda g: g.reshape(1, -1)

    ab_re, ab_im, bbt_re, bbt_im = _ssm_prep(ssm_a_re, ssm_a_im, ssm_log_dt, ssm_b_re, ssm_b_im)
    wb = jnp.concatenate([_block_diag_tiles(bbt_re), _block_diag_tiles(bbt_im)], axis=-1).astype(BF16)
    wc = jnp.concatenate([_block_diag_tiles(jnp.transpose(ssm_c_re, (0, 2, 1))),
                          _block_diag_tiles(-jnp.transpose(ssm_c_im, (0, 2, 1)))], axis=1).astype(BF16)
    a_cat = jnp.stack([ab_re.reshape(-1), ab_im.reshape(-1)], axis=0)
    a_cat = jnp.broadcast_to(a_cat[:, None, :], (2, SUBLANES, a_cat.shape[1]))
    d_row = ssm_d.reshape(1, ssm_w)

    u_tm, qt, ka, vt, sg = _proj(x, mod3, row(g_pre_mix), w_in, n_heads, ssm_w, tm)

    zero_state = jnp.zeros((2, SUBLANES, n_groups * n_state), F32)
    first = _ssm_pass(u_tm, wb, a_cat, wc, d_row, zero_state, steps_per_tile, emit=False)
    init = jnp.concatenate([jnp.zeros_like(first[:, :bsz]), first[:, :bsz]], axis=1)
    z_tm, _ = _ssm_pass(u_tm, wb, a_cat, wc, d_row, init, steps_per_tile, emit=True)

    o = _attention(qt, ka, vt)

    x1 = _merge(x, z_tm, o, sg, mod3, w_glu_a.astype(BF16), w_glu_b.astype(BF16),
                w_attn_out.astype(BF16), w_out.astype(BF16), row(g_post_mix), tm)
    return _ffn(x1, mod3, row(g_pre_ffn), w_ff_gate.astype(BF16), w_ff_up.astype(BF16),
                w_ff_down.astype(BF16), row(g_post_ffn), tm)


def kernel(x, c, w_ada, b_ada, g_pre_mix, g_post_mix, w_in, ssm_a_re, ssm_a_im, ssm_log_dt, ssm_b_re, ssm_b_im, ssm_c_re, ssm_c_im, ssm_d, w_glu_a, w_glu_b, w_attn_out, w_out, g_pre_ffn, g_post_ffn, w_ff_gate, w_ff_up, w_ff_down):
    depth = w_ada.shape[0]
    for l in range(depth):
        mod = _adaln(c, w_ada[l], b_ada[l])
        x = _layer(x, mod, g_pre_mix[l], g_post_mix[l], w_in[l], ssm_a_re[l], ssm_a_im[l], ssm_log_dt[l],
                   ssm_b_re[l], ssm_b_im[l], ssm_c_re[l], ssm_c_im[l], ssm_d[l], w_glu_a[l], w_glu_b[l],
                   w_attn_out[l], w_out[l], g_pre_ffn[l], g_post_ffn[l], w_ff_gate[l], w_ff_up[l], w_ff_down[l])
    return x
```
